```python
import jax, jax.numpy as jnp
from jax import lax
import numpy as np

D_MODEL = 1024
BATCH = 16
SEQ = 4096
DEPTH = 4

MIX_WIDTH = D_MODEL // 4
HEAD_DIM = 64
N_BRANCH = 4

A_WIDTH = MIX_WIDTH
A_GROUPS = A_WIDTH // HEAD_DIM
A_GROUP_DIM = A_WIDTH // A_GROUPS
A_CHUNK = 128

B_HEADS = MIX_WIDTH // HEAD_DIM
B_HEAD_DIM = HEAD_DIM
B_ROT_DIM = B_HEAD_DIM // 4
MOBA_BLOCK = 256
MOBA_TOPK = 3
MOBA_QBLK = 32

C_WIDTH = MIX_WIDTH
CONV_WIDTH = 3

D_HEADS = MIX_WIDTH // HEAD_DIM
D_NOPE = 64
D_ROPE = 32
D_V = 64
D_Q_LORA = 192
D_KV_LORA = 128
MLA_QBLK = 128

D_FF = 4 * D_MODEL
ROPE_THETA = 500000.0
NORM_EPS = 1e-6

SPLIT_SIZES = (A_WIDTH, A_WIDTH,
               B_HEADS * B_HEAD_DIM, B_HEADS * B_HEAD_DIM, B_HEADS * B_HEAD_DIM,
               C_WIDTH, C_WIDTH, C_WIDTH,
               D_Q_LORA, D_KV_LORA, D_ROPE,
               N_BRANCH * D_MODEL)
IN_COLS = (2 * A_WIDTH + 3 * B_HEADS * B_HEAD_DIM + 3 * C_WIDTH
           + D_Q_LORA + D_KV_LORA + D_ROPE + N_BRANCH * D_MODEL)

kernel_name = 'hybrid_gated_parallel_sgu_moba_shortconv_mla'

F32 = jnp.float32


def rms_norm(x, g):
    xf = x.astype(F32)
    y = xf * lax.rsqrt(jnp.mean(xf * xf, axis=-1, keepdims=True) + NORM_EPS)
    return (y * g.astype(F32)).astype(x.dtype)


def layer_norm(x, g, b):
    xf = x.astype(F32)
    mu = jnp.mean(xf, axis=-1, keepdims=True)
    xc = xf - mu
    var = jnp.mean(xc * xc, axis=-1, keepdims=True)
    return (xc * lax.rsqrt(var + 1e-5) * g.astype(F32) + b.astype(F32)).astype(x.dtype)


def rope_tables(dim, seq):
    inv = 1.0 / (ROPE_THETA ** (jnp.arange(0, dim, 2, dtype=F32) / dim))
    ang = jnp.arange(seq, dtype=F32)[:, None] * inv[None, :]
    return jnp.cos(ang), jnp.sin(ang)


def apply_rope(x, cos, sin):
    half = x.shape[-1] // 2
    x1 = x[..., :half].astype(F32)
    x2 = x[..., half:].astype(F32)
    c = cos[:, None, :]
    s = sin[:, None, :]
    return jnp.concatenate([x1 * c - x2 * s, x2 * c + x1 * s], axis=-1).astype(x.dtype)


def partial_rope(x, cos, sin):
    return jnp.concatenate([apply_rope(x[..., :B_ROT_DIM], cos, sin), x[..., B_ROT_DIM:]], axis=-1)


def split_columns(proj):
    parts = []
    start = 0
    for n in SPLIT_SIZES:
        parts.append(proj[..., start:start + n])
        start += n
    return parts


def chunked_sgu(u, v, ln_g, ln_b, w_s, b_s):
    bsz, s_len, _ = v.shape
    v = layer_norm(v, ln_g, ln_b)
    vc = v.reshape(bsz, s_len // A_CHUNK, A_CHUNK, A_GROUPS, A_GROUP_DIM)
    causal = jnp.arange(A_CHUNK)[:, None] >= jnp.arange(A_CHUNK)[None, :]
    w = jnp.where(causal[None], w_s, jnp.zeros((), w_s.dtype))
    sv = jnp.einsum('gts,bcsgd->bctgd', w, vc) + b_s.T[None, None, :, :, None]
    return u * sv.reshape(bsz, s_len, A_WIDTH)


def moba_attention(q, k, v):
    bsz, s_len, n_h, dh = q.shape
    s_pad = -(-s_len // MOBA_BLOCK) * MOBA_BLOCK
    pad = ((0, 0), (0, s_pad - s_len), (0, 0), (0, 0))
    qh = q.transpose(0, 2, 1, 3)
    kh = jnp.pad(k, pad).transpose(0, 2, 1, 3)
    vh = jnp.pad(v, pad).transpose(0, 2, 1, 3)
    nb = s_pad // MOBA_BLOCK
    n_cand = max(nb, MOBA_TOPK)
    kblk = kh.reshape(bsz, n_h, nb, MOBA_BLOCK, dh)
    vblk = vh.reshape(bsz, n_h, nb, MOBA_BLOCK, dh)
    kmean = jnp.mean(kblk.astype(F32), axis=3)
    scale = dh ** -0.5
    gather = jax.vmap(jax.vmap(lambda blocks, idx: blocks[idx]))
    kl = MOBA_TOPK * MOBA_BLOCK

    def step(i):
        q0 = i * MOBA_QBLK
        qb = lax.dynamic_slice_in_dim(qh, q0, MOBA_QBLK, axis=2)
        cur = q0 // MOBA_BLOCK
        gate = jnp.einsum('bhqd,bhnd->bhqn', qb.astype(F32), kmean)
        if n_cand > nb:
            gate = jnp.pad(gate, ((0, 0), (0, 0), (0, 0), (0, n_cand - nb)))
        gate = jnp.where(jnp.arange(n_cand) < cur, gate, -jnp.inf)
        _, idx = lax.top_k(gate, MOBA_TOPK)
        idx = jnp.minimum(idx, nb - 1)
        valid = jnp.arange(MOBA_TOPK) < cur
        ksel = gather(kblk, idx)
        vsel = gather(vblk, idx)
        s_sel = jnp.einsum('bhqd,bhqkld->bhqkl', qb, ksel, preferred_element_type=F32) * scale
        s_sel = jnp.where(valid[:, None], s_sel, -jnp.inf).reshape(bsz, n_h, MOBA_QBLK, kl)
        kown = lax.dynamic_slice_in_dim(kh, cur * MOBA_BLOCK, MOBA_BLOCK, axis=2)
        vown = lax.dynamic_slice_in_dim(vh, cur * MOBA_BLOCK, MOBA_BLOCK, axis=2)
        s_own = jnp.einsum('bhqd,bhld->bhql', qb, kown, preferred_element_type=F32) * scale
        qpos = q0 + jnp.arange(MOBA_QBLK)
        kpos = cur * MOBA_BLOCK + jnp.arange(MOBA_BLOCK)
        s_own = jnp.where(kpos[None, :] <= qpos[:, None], s_own, -jnp.inf)
        p = jax.nn.softmax(jnp.concatenate([s_sel, s_own], axis=-1), axis=-1)
        p_sel = p[..., :kl].reshape(bsz, n_h, MOBA_QBLK, MOBA_TOPK, MOBA_BLOCK).astype(v.dtype)
        p_own = p[..., kl:].astype(v.dtype)
        return (jnp.einsum('bhqkl,bhqkld->bhqd', p_sel, vsel)
                + jnp.einsum('bhql,bhld->bhqd', p_own, vown))

    outs = lax.map(step, jnp.arange(s_len // MOBA_QBLK))
    return outs.transpose(1, 0, 3, 2, 4).reshape(bsz, s_len, n_h * dh)


def short_conv_mixer(bg, cg, xin, w_conv):
    z = cg * xin
    z = lax.conv_general_dilated(z, w_conv[:, None, :], window_strides=(1,),
                                 padding=[(CONV_WIDTH - 1, 0)],
                                 dimension_numbers=('NWC', 'WIO', 'NWC'),
                                 feature_group_count=C_WIDTH)
    return bg * z


def mla_attention(cq, ckv, kr, q_norm_g, w_uq, kv_norm_g, w_ukv, cos, sin):
    bsz, s_len, _ = cq.shape
    q = jnp.einsum('bsr,rhe->bshe', rms_norm(cq, q_norm_g), w_uq)
    q_nope = q[..., :D_NOPE]
    q_rope = apply_rope(q[..., D_NOPE:], cos, sin)
    kv = jnp.einsum('bsr,rhe->bshe', rms_norm(ckv, kv_norm_g), w_ukv)
    k_nope = kv[..., :D_NOPE]
    v = kv[..., D_NOPE:]
    k_rope = apply_rope(kr[:, :, None, :], cos, sin)[:, :, 0, :]
    scale = (D_NOPE + D_ROPE) ** -0.5
    kpos = jnp.arange(s_len)

    def step(i):
        q0 = i * MLA_QBLK
        qn = lax.dynamic_slice_in_dim(q_nope, q0, MLA_QBLK, axis=1)
        qr = lax.dynamic_slice_in_dim(q_rope, q0, MLA_QBLK, axis=1)
        s = (jnp.einsum('bqhn,bkhn->bhqk', qn, k_nope, preferred_element_type=F32)
             + jnp.einsum('bqhr,bkr->bhqk', qr, k_rope, preferred_element_type=F32)) * scale
        qpos = q0 + jnp.arange(MLA_QBLK)
        s = jnp.where(kpos[None, :] <= qpos[:, None], s, -jnp.inf)
        p = jax.nn.softmax(s, axis=-1).astype(v.dtype)
        return jnp.einsum('bhqk,bkhd->bqhd', p, v)

    outs = lax.map(step, jnp.arange(s_len // MLA_QBLK))
    return outs.transpose(1, 0, 2, 3, 4).reshape(bsz, s_len, D_HEADS * D_V)


def hybrid_layer(x, g_pre_mix, w_in, a_ln_g, a_ln_b, a_w_s, a_b_s, c_w_conv,
                 d_q_norm_g, d_w_uq, d_kv_norm_g, d_w_ukv, w_branch, w_out, g_post_mix,
                 g_pre_mlp, w_mlp_in, w_mlp_out, g_post_mlp, cos_b, sin_b, cos_d, sin_d):
    bsz, s_len, _ = x.shape
    h = rms_norm(x, g_pre_mix)
    (a_u, a_v, b_q, b_k, b_v, c_b, c_c, c_x,
     d_cq, d_ckv, d_kr, gate_logits) = split_columns(h @ w_in)

    out_a = chunked_sgu(jax.nn.gelu(a_u), jax.nn.gelu(a_v), a_ln_g, a_ln_b, a_w_s, a_b_s)
    hs = (bsz, s_len, B_HEADS, B_HEAD_DIM)
    out_b = moba_attention(partial_rope(b_q.reshape(hs), cos_b, sin_b),
                           partial_rope(b_k.reshape(hs), cos_b, sin_b),
                           b_v.reshape(hs))
    out_c = short_conv_mixer(c_b, c_c, c_x, c_w_conv)
    out_d = mla_attention(d_cq, d_ckv, d_kr, d_q_norm_g, d_w_uq, d_kv_norm_g, d_w_ukv, cos_d, sin_d)

    gates = jax.nn.sigmoid(gate_logits.reshape(bsz, s_len, N_BRANCH, D_MODEL))
    branches = (out_a, out_b, out_c, out_d)
    merged = gates[:, :, 0, :] * (branches[0] @ w_branch[0])
    for i in range(1, N_BRANCH):
        merged = merged + gates[:, :, i, :] * (branches[i] @ w_branch[i])
    x = x + rms_norm(merged @ w_out, g_post_mix)

    h = rms_norm(x, g_pre_mlp)
    y = jnp.square(jax.nn.relu(h @ w_mlp_in)) @ w_mlp_out
    return x + rms_norm(y, g_post_mlp)


def setup_inputs(seed: int = 0) -> dict:
    key = jax.random.key(seed)
    ks = jax.random.split(key, 20)
    L = DEPTH

    def nrm(k, shape, fan_in):
        return jax.random.normal(k, shape, F32) * (fan_in ** -0.5)

    def gain(k, shape):
        return 1.0 + 0.05 * jax.random.normal(k, shape, F32)

    return {
        'x': jax.random.normal(ks[0], (BATCH, SEQ, D_MODEL), F32),
        'g_pre_mix': gain(ks[1], (L, D_MODEL)),
        'w_in': nrm(ks[2], (L, D_MODEL, IN_COLS), D_MODEL),
        'a_ln_g': gain(ks[3], (L, A_WIDTH)),
        'a_ln_b': 0.02 * jax.random.normal(ks[4], (L, A_WIDTH), F32),
        'a_w_s': nrm(ks[5], (L, A_GROUPS, A_CHUNK, A_CHUNK), A_CHUNK),
        'a_b_s': gain(ks[6], (L, A_GROUPS, A_CHUNK)),
        'c_w_conv': nrm(ks[7], (L, CONV_WIDTH, C_WIDTH), CONV_WIDTH),
        'd_q_norm_g': gain(ks[8], (L, D_Q_LORA)),
        'd_w_uq': nrm(ks[9], (L, D_Q_LORA, D_HEADS, D_NOPE + D_ROPE), D_Q_LORA),
        'd_kv_norm_g': gain(ks[10], (L, D_KV_LORA)),
        'd_w_ukv': nrm(ks[11], (L, D_KV_LORA, D_HEADS, D_NOPE + D_V), D_KV_LORA),
        'w_branch': nrm(ks[12], (L, N_BRANCH, MIX_WIDTH, D_MODEL), MIX_WIDTH),
        'w_out': nrm(ks[13], (L, D_MODEL, D_MODEL), D_MODEL),
        'g_post_mix': gain(ks[14], (L, D_MODEL)),
        'g_pre_mlp': gain(ks[15], (L, D_MODEL)),
        'w_mlp_in': nrm(ks[16], (L, D_MODEL, D_FF), D_MODEL),
        'w_mlp_out': nrm(ks[17], (L, D_FF, D_MODEL), D_FF),
        'g_post_mlp': gain(ks[18], (L, D_MODEL)),
    }


def reference(x, g_pre_mix, w_in, a_ln_g, a_ln_b, a_w_s, a_b_s, c_w_conv,
              d_q_norm_g, d_w_uq, d_kv_norm_g, d_w_ukv, w_branch, w_out, g_post_mix,
              g_pre_mlp, w_mlp_in, w_mlp_out, g_post_mlp):
    s_len = x.shape[1]
    cos_b, sin_b = rope_tables(B_ROT_DIM, s_len)
    cos_d, sin_d = rope_tables(D_ROPE, s_len)
    for l in range(DEPTH):
        x = hybrid_layer(x, g_pre_mix[l], w_in[l], a_ln_g[l], a_ln_b[l], a_w_s[l], a_b_s[l],
                         c_w_conv[l], d_q_norm_g[l], d_w_uq[l], d_kv_norm_g[l], d_w_ukv[l],
                         w_branch[l], w_out[l], g_post_mix[l], g_pre_mlp[l], w_mlp_in[l],
                         w_mlp_out[l], g_post_mlp[l], cos_b, sin_b, cos_d, sin_d)
    return x
```

```python
import functools
import math

import numpy as np
import jax
import jax.numpy as jnp
from jax import lax
from jax.experimental import pallas as pl
from jax.experimental.pallas import tpu as pltpu

F32 = jnp.float32
BF16 = jnp.bfloat16

D_MODEL = 1024
HEAD_DIM = 64
N_HEADS = 4
MIX = 256
A_CHUNK = 128
MOBA_BLOCK = 256
MOBA_TOPK = 3
B_ROT = 16
D_NOPE, D_ROPE, D_Q_LORA, D_KV_LORA = 64, 32, 192, 128
ROPE_THETA = 500000.0
NORM_EPS = 1e-6
LN_EPS = 1e-5
LOG2E = 1.4426950408889634

LANE = 128
TM_IN = 512
TM_MERGE = 256
TM_MLP = 512
TQ = 256
TK = 256
VMEM_LIMIT = 56 * 1024 * 1024

N1 = 2560
NEG_INF = float("-inf")


def _moba_perm():
    perm = np.zeros(MIX, np.int32)
    for h in range(N_HEADS):
        for j in range(8):
            perm[h * 8 + j] = h * HEAD_DIM + j
            perm[LANE + h * 8 + j] = h * HEAD_DIM + 8 + j
        for r in range(24):
            perm[32 + h * 24 + r] = h * HEAD_DIM + 16 + r
            perm[LANE + 32 + h * 24 + r] = h * HEAD_DIM + 40 + r
    return perm


def _moba_head_mask(idx, h):
    l = idx & (LANE - 1)
    return ((l >= 8 * h) & (l < 8 * h + 8)) | ((l >= 32 + 24 * h) & (l < 56 + 24 * h))


def _rms(x, g):
    return x * lax.rsqrt(jnp.mean(x * x, axis=-1, keepdims=True) + NORM_EPS) * g


def _gelu(x):
    return 0.5 * x * (1.0 + jnp.tanh(math.sqrt(2.0 / math.pi) * (x + 0.044715 * (x * x * x))))


def _const_spec(shape):
    nd = len(shape)
    return pl.BlockSpec(shape, lambda *_: (0,) * nd, pipeline_mode=pl.Buffered(1))


def _inproj_kernel(x_ref, g_ref, w1_ref, tab_ref, gq_ref, gkv_ref, wq_ref, wkv_ref,
                   auv_ref, bqh_ref, bql_ref, bk_ref, bvt_ref, km_ref, c3_ref,
                   dqt_ref, dk_ref, dvt_ref):
    tm = x_ref.shape[0]
    h = _rms(x_ref[...], g_ref[...]).astype(BF16)

    def proj(a, b):
        return jnp.dot(h, w1_ref[:, a:b], preferred_element_type=F32)

    auv_ref[...] = proj(0, 512)

    cb = tab_ref[:, 0:128]
    sb = tab_ref[:, 128:256]
    q0, q1 = proj(512, 640), proj(640, 768)
    q = jnp.concatenate([q0 * cb - q1 * sb, q1 * cb + q0 * sb], axis=1)
    k0, k1 = proj(768, 896), proj(896, 1024)
    k = jnp.concatenate([k0 * cb - k1 * sb, k1 * cb + k0 * sb], axis=1)
    v = proj(1024, 1280)
    qt = q.T
    qt_hi = qt.astype(BF16)
    qt_lo = (qt - qt_hi.astype(F32)).astype(BF16)
    vt = v.T.astype(BF16)
    bk_ref[...] = k.astype(BF16)
    for j in range(tm // TQ):
        bqh_ref[j] = qt_hi[:, j * TQ:(j + 1) * TQ]
        bql_ref[j] = qt_lo[:, j * TQ:(j + 1) * TQ]
        bvt_ref[j] = vt[:, j * TK:(j + 1) * TK]
    for j in range(tm // MOBA_BLOCK):
        blk = k[j * MOBA_BLOCK:(j + 1) * MOBA_BLOCK, :]
        km_ref[0, j:j + 1, :] = jnp.sum(blk, axis=0, keepdims=True) * (1.0 / MOBA_BLOCK)

    c3_ref[...] = proj(1280, 2048)

    d = proj(2048, 2560)
    cq = d[:, 0:256]
    lane = lax.broadcasted_iota(jnp.int32, cq.shape, 1)
    ms = jnp.sum(jnp.where(lane < D_Q_LORA, cq * cq, 0.0), axis=-1, keepdims=True) * (1.0 / D_Q_LORA)
    cqn = (cq * lax.rsqrt(ms + NORM_EPS) * gq_ref[...]).astype(BF16)
    qq = jnp.dot(cqn, wq_ref[...], preferred_element_type=F32)
    cq_t = jnp.concatenate([tab_ref[:, 256:384]] * N_HEADS, axis=1)
    sq_t = jnp.concatenate([tab_ref[:, 384:512]] * N_HEADS, axis=1)
    qd = qq[:, 0:512] * cq_t + qq[:, 512:1024] * sq_t
    qdt = qd.T.astype(BF16)

    kvn = _rms(d[:, 384:512], gkv_ref[...]).astype(BF16)
    kv = jnp.dot(kvn, wkv_ref[...], preferred_element_type=F32)
    kr = d[:, 128:256] * tab_ref[:, 512:640] + d[:, 256:384] * tab_ref[:, 384:512]
    dk_ref[...] = (kv[:, 0:512] + jnp.concatenate([kr] * N_HEADS, axis=1)).astype(BF16)
    vdt = kv[:, 512:768].T.astype(BF16)
    for j in range(tm // TQ):
        dqt_ref[j] = qdt[:, j * TQ:(j + 1) * TQ]
        dvt_ref[j] = vdt[:, j * TK:(j + 1) * TK]


def _inproj(x2, g, w1, tab, gq, gkv, wq, wkv, seq):
    t = x2.shape[0]
    tm = TM_IN
    nt = t // tm
    spt = seq // tm
    rows = lambda w: pl.BlockSpec((tm, w), lambda i: (i, 0))
    tblk = lambda r: pl.BlockSpec((tm // TQ, r, TQ), lambda i: (i, 0, 0))
    out_shape = (
        jax.ShapeDtypeStruct((t, 512), F32),
        jax.ShapeDtypeStruct((t // TQ, MIX, TQ), BF16),
        jax.ShapeDtypeStruct((t // TQ, MIX, TQ), BF16),
        jax.ShapeDtypeStruct((t, MIX), BF16),
        jax.ShapeDtypeStruct((t // TK, MIX, TK), BF16),
        jax.ShapeDtypeStruct((nt, tm // MOBA_BLOCK, MIX), F32),
        jax.ShapeDtypeStruct((t, 768), F32),
        jax.ShapeDtypeStruct((t // TQ, 512, TQ), BF16),
        jax.ShapeDtypeStruct((t, 512), BF16),
        jax.ShapeDtypeStruct((t // TK, MIX, TK), BF16),
    )
    out_specs = (
        rows(512), tblk(MIX), tblk(MIX), rows(MIX), tblk(MIX),
        pl.BlockSpec((1, tm // MOBA_BLOCK, MIX), lambda i: (i, 0, 0)),
        rows(768), tblk(512), rows(512), tblk(MIX),
    )
    in_specs = [
        rows(D_MODEL), _const_spec((1, D_MODEL)), _const_spec((D_MODEL, N1)),
        pl.BlockSpec((tm, 640), lambda i: (i % spt, 0)),
        _const_spec((1, 256)), _const_spec((1, 128)),
        _const_spec((256, 1024)), _const_spec((128, 768)),
    ]
    return pl.pallas_call(
        _inproj_kernel, grid=(nt,), in_specs=in_specs, out_specs=out_specs, out_shape=out_shape,
        compiler_params=pltpu.CompilerParams(dimension_semantics=("arbitrary",),
                                             vmem_limit_bytes=VMEM_LIMIT),
        name="inproj",
    )(x2, g, w1, tab, gq, gkv, wq, wkv)


def _first_tile(s, vt, c):
    m = jnp.max(s, axis=0, keepdims=True)
    p = jnp.exp2((s - m) * c)
    l = jnp.sum(p, axis=0, keepdims=True)
    acc = jnp.dot(vt, p.astype(BF16), preferred_element_type=F32)
    return m, l, acc


def _next_tile(s, vt, c, m, l, acc):
    m_new = jnp.maximum(m, jnp.max(s, axis=0, keepdims=True))
    alpha = jnp.exp2((m - m_new) * c)
    p = jnp.exp2((s - m_new) * c)
    l_new = alpha * l + jnp.sum(p, axis=0, keepdims=True)
    acc_new = alpha * acc + jnp.dot(vt, p.astype(BF16), preferred_element_type=F32)
    return m_new, l_new, acc_new


def _causal_keep():
    row = lax.broadcasted_iota(jnp.int32, (TK, TQ), 0)
    col = lax.broadcasted_iota(jnp.int32, (TK, TQ), 1)
    return row <= col


def _moba_kernel(qh_ref, ql_ref, k_ref, vt_ref, km_ref, o_ref, m_s, l_s, acc_s, bias_s, qm_s):
    i = pl.program_id(1)
    c = (HEAD_DIM ** -0.5) * LOG2E
    qt_hi = qh_ref[0]
    qt_lo = ql_ref[0]

    km = km_ref[0]
    nb = km.shape[0]
    lane = lax.broadcasted_iota(jnp.int32, km.shape, 1)
    kmh = jnp.concatenate([jnp.where(_moba_head_mask(lane, h), km, 0.0) for h in range(N_HEADS)], axis=0)
    km_hi = kmh.astype(BF16)
    km_lo = (kmh - km_hi.astype(F32)).astype(BF16)
    gate = (jnp.dot(km_hi, qt_hi, preferred_element_type=F32)
            + jnp.dot(km_hi, qt_lo, preferred_element_type=F32)
            + jnp.dot(km_lo, qt_hi, preferred_element_type=F32))

    blk = lax.broadcasted_iota(jnp.int32, (nb, TQ), 0)
    blkf = blk.astype(F32)
    for h in range(N_HEADS):
        g = jnp.where(blk < i, gate[h * nb:(h + 1) * nb, :], NEG_INF)
        bias = jnp.full((nb, TQ), NEG_INF, F32)
        for _ in range(MOBA_TOPK):
            mx = jnp.max(g, axis=0, keepdims=True)
            first = jnp.min(jnp.where(g == mx, blkf, float(nb)), axis=0, keepdims=True)
            hit = blkf == first
            bias = jnp.where(hit & (mx > NEG_INF), 0.0, bias)
            g = jnp.where(hit, NEG_INF, g)
        bias_s[h * nb:(h + 1) * nb, :] = bias

    row = lax.broadcasted_iota(jnp.int32, (MIX, TQ), 0)
    for h in range(N_HEADS):
        qm_s[h] = jnp.where(_moba_head_mask(row, h), qt_hi, jnp.zeros_like(qt_hi))
    keep = _causal_keep()
    q0 = pl.multiple_of(i * TQ, TQ)
    kd = k_ref[pl.ds(q0, TK), :]
    for h in range(N_HEADS):
        s = jnp.dot(kd, qm_s[h], preferred_element_type=F32)
        s = jnp.where(keep, s, NEG_INF)
        m, l, acc = _first_tile(s, vt_ref[i, h * HEAD_DIM:(h + 1) * HEAD_DIM, :], c)
        m_s[h:h + 1, :] = m
        l_s[h:h + 1, :] = l
        acc_s[h * HEAD_DIM:(h + 1) * HEAD_DIM, :] = acc

    def body(n, carry):
        k0 = pl.multiple_of(n * TK, TK)
        kn = k_ref[pl.ds(k0, TK), :]
        for h in range(N_HEADS):
            s = jnp.dot(kn, qm_s[h], preferred_element_type=F32) + bias_s[pl.ds(h * nb + n, 1), :]
            sl = slice(h * HEAD_DIM, (h + 1) * HEAD_DIM)
            m, l, acc = _next_tile(s, vt_ref[n, sl, :], c, m_s[h:h + 1, :], l_s[h:h + 1, :], acc_s[sl, :])
            m_s[h:h + 1, :] = m
            l_s[h:h + 1, :] = l
            acc_s[sl, :] = acc
        return carry

    lax.fori_loop(0, i, body, 0)

    inv = jnp.concatenate(
        [jnp.broadcast_to(1.0 / l_s[h:h + 1, :], (HEAD_DIM, TQ)) for h in range(N_HEADS)], axis=0)
    o_ref[...] = (acc_s[...] * inv).T.astype(o_ref.dtype)


def _moba(qh, ql, k, vt, km, batch, seq):
    nq = seq // TQ
    t = batch * seq
    qspec = pl.BlockSpec((1, MIX, TQ), lambda b, i: (b * nq + i, 0, 0))
    return pl.pallas_call(
        _moba_kernel, grid=(batch, nq),
        in_specs=[qspec, qspec,
                  pl.BlockSpec((seq, MIX), lambda b, i: (b, 0)),
                  pl.BlockSpec((seq // TK, MIX, TK), lambda b, i: (b, 0, 0)),
                  pl.BlockSpec((1, seq // MOBA_BLOCK, MIX), lambda b, i: (b, 0, 0))],
        out_specs=pl.BlockSpec((TQ, MIX), lambda b, i: (b * nq + i, 0)),
        out_shape=jax.ShapeDtypeStruct((t, MIX), BF16),
        scratch_shapes=[pltpu.VMEM((8, TQ), F32), pltpu.VMEM((8, TQ), F32),
                        pltpu.VMEM((MIX, TQ), F32),
                        pltpu.VMEM((N_HEADS * (seq // MOBA_BLOCK), TQ), F32),
                        pltpu.VMEM((N_HEADS, MIX, TQ), BF16)],
        compiler_params=pltpu.CompilerParams(dimension_semantics=("arbitrary", "arbitrary"),
                                             vmem_limit_bytes=VMEM_LIMIT),
        name="moba",
    )(qh, ql, k, vt, km)


def _mla_kernel(qt_ref, k_ref, vt_ref, o_ref, m_s, l_s, acc_s):
    i = pl.program_id(1)
    c = ((D_NOPE + D_ROPE) ** -0.5) * LOG2E
    keep = _causal_keep()
    q0 = pl.multiple_of(i * TQ, TQ)
    for h in range(N_HEADS):
        qh = qt_ref[0, h * LANE:(h + 1) * LANE, :]
        s = jnp.dot(k_ref[pl.ds(q0, TK), h * LANE:(h + 1) * LANE], qh, preferred_element_type=F32)
        s = jnp.where(keep, s, NEG_INF)
        m, l, acc = _first_tile(s, vt_ref[i, h * HEAD_DIM:(h + 1) * HEAD_DIM, :], c)
        m_s[h:h + 1, :] = m
        l_s[h:h + 1, :] = l
        acc_s[h * HEAD_DIM:(h + 1) * HEAD_DIM, :] = acc

    def body(n, carry):
        k0 = pl.multiple_of(n * TK, TK)
        for h in range(N_HEADS):
            qh = qt_ref[0, h * LANE:(h + 1) * LANE, :]
            s = jnp.dot(k_ref[pl.ds(k0, TK), h * LANE:(h + 1) * LANE], qh, preferred_element_type=F32)
            sl = slice(h * HEAD_DIM, (h + 1) * HEAD_DIM)
            m, l, acc = _next_tile(s, vt_ref[n, sl, :], c, m_s[h:h + 1, :], l_s[h:h + 1, :], acc_s[sl, :])
            m_s[h:h + 1, :] = m
            l_s[h:h + 1, :] = l
            acc_s[sl, :] = acc
        return carry

    lax.fori_loop(0, i, body, 0)

    inv = jnp.concatenate(
        [jnp.broadcast_to(1.0 / l_s[h:h + 1, :], (HEAD_DIM, TQ)) for h in range(N_HEADS)], axis=0)
    o_ref[...] = (acc_s[...] * inv).T.astype(o_ref.dtype)


def _mla(qt, k, vt, batch, seq):
    nq = seq // TQ
    t = batch * seq
    return pl.pallas_call(
        _mla_kernel, grid=(batch, nq),
        in_specs=[pl.BlockSpec((1, 512, TQ), lambda b, i: (b * nq + i, 0, 0)),
                  pl.BlockSpec((seq, 512), lambda b, i: (b, 0)),
                  pl.BlockSpec((seq // TK, MIX, TK), lambda b, i: (b, 0, 0))],
        out_specs=pl.BlockSpec((TQ, MIX), lambda b, i: (b * nq + i, 0)),
        out_shape=jax.ShapeDtypeStruct((t, MIX), BF16),
        scratch_shapes=[pltpu.VMEM((8, TQ), F32), pltpu.VMEM((8, TQ), F32),
                        pltpu.VMEM((MIX, TQ), F32)],
        compiler_params=pltpu.CompilerParams(dimension_semantics=("arbitrary", "arbitrary"),
                                             vmem_limit_bytes=VMEM_LIMIT),
        name="mla",
    )(qt, k, vt)


def _merge_kernel(x_ref, auv_ref, c3_ref, c3p_ref, ob_ref, od_ref,
                  g_ref, wg_ref, lng_ref, lnb_ref, ws_ref, bs_ref, wc_ref, wb_ref, wo_ref, gp_ref,
                  o_ref, *, steps_per_seq):
    tm = x_ref.shape[0]
    x = x_ref[...]
    h = _rms(x, g_ref[...]).astype(BF16)

    u = _gelu(auv_ref[:, 0:MIX])
    v = _gelu(auv_ref[:, MIX:2 * MIX])
    mu = jnp.mean(v, axis=-1, keepdims=True)
    vc = v - mu
    var = jnp.mean(vc * vc, axis=-1, keepdims=True)
    v = (vc * lax.rsqrt(var + LN_EPS) * lng_ref[...] + lnb_ref[...]).astype(BF16)
    wrow = lax.broadcasted_iota(jnp.int32, ws_ref.shape, 0) & (A_CHUNK - 1)
    wcol = lax.broadcasted_iota(jnp.int32, ws_ref.shape, 1)
    ws = jnp.where(wrow >= wcol, ws_ref[...], 0.0).astype(BF16)
    lane = lax.broadcasted_iota(jnp.int32, (A_CHUNK, MIX), 1)
    svs = []
    for cidx in range(tm // A_CHUNK):
        r = jnp.dot(ws, v[cidx * A_CHUNK:(cidx + 1) * A_CHUNK, :], preferred_element_type=F32)
        sv = jnp.where(lane < 64, r[0:128],
                       jnp.where(lane < 128, r[128:256], jnp.where(lane < 192, r[256:384], r[384:512])))
        svs.append(sv + bs_ref[...])
    out_a = (u * jnp.concatenate(svs, axis=0)).astype(BF16)

    z = c3_ref[:, MIX:2 * MIX] * c3_ref[:, 2 * MIX:3 * MIX]
    zp = c3p_ref[:, MIX:2 * MIX] * c3p_ref[:, 2 * MIX:3 * MIX]
    zp = jnp.where(pl.program_id(0) % steps_per_seq == 0, 0.0, zp)
    zm1, zm2 = zp[7:8, :], zp[6:7, :]
    rowi = lax.broadcasted_iota(jnp.int32, z.shape, 0)
    z1 = jnp.where(rowi == 0, zm1, pltpu.roll(z, 1, axis=0))
    z2 = jnp.where(rowi == 0, zm2, jnp.where(rowi == 1, zm1, pltpu.roll(z, 2, axis=0)))
    conv = wc_ref[0:1, :] * z2 + wc_ref[1:2, :] * z1 + wc_ref[2:3, :] * z
    out_c = (c3_ref[:, 0:MIX] * conv).astype(BF16)

    branches = (out_a, ob_ref[...], out_c, od_ref[...])
    merged = None
    for bi in range(4):
        logits = jnp.dot(h, wg_ref[:, bi * D_MODEL:(bi + 1) * D_MODEL], preferred_element_type=F32)
        gate = 1.0 / (1.0 + jnp.exp(-logits))
        term = gate * jnp.dot(branches[bi], wb_ref[bi], preferred_element_type=F32)
        merged = term if merged is None else merged + term
    y = jnp.dot(merged.astype(BF16), wo_ref[...], preferred_element_type=F32)
    o_ref[...] = x + _rms(y, gp_ref[...])


def _merge(x2, auv, c3, ob, od, g, wg, lng, lnb, ws, bs, wc, wb, wo, gp, seq):
    t = x2.shape[0]
    tm = TM_MERGE
    rows = lambda w: pl.BlockSpec((tm, w), lambda i: (i, 0))
    prev8 = pl.BlockSpec((8, 768), lambda i: (jnp.maximum(i * (tm // 8) - 1, 0), 0))
    in_specs = [
        rows(D_MODEL), rows(512), rows(768), prev8, rows(MIX), rows(MIX),
        _const_spec((1, D_MODEL)), _const_spec((D_MODEL, 4 * D_MODEL)),
        _const_spec((1, MIX)), _const_spec((1, MIX)),
        _const_spec((N_HEADS * A_CHUNK, A_CHUNK)), _const_spec((A_CHUNK, MIX)),
        _const_spec((3, MIX)), _const_spec((4, MIX, D_MODEL)),
        _const_spec((D_MODEL, D_MODEL)), _const_spec((1, D_MODEL)),
    ]
    return pl.pallas_call(
        functools.partial(_merge_kernel, steps_per_seq=seq // tm), grid=(t // tm,),
        in_specs=in_specs, out_specs=rows(D_MODEL),
        out_shape=jax.ShapeDtypeStruct((t, D_MODEL), F32),
        compiler_params=pltpu.CompilerParams(dimension_semantics=("arbitrary",),
                                             vmem_limit_bytes=VMEM_LIMIT),
        name="merge",
    )(x2, auv, c3, c3, ob, od, g, wg, lng, lnb, ws, bs, wc, wb, wo, gp)


def _mlp_kernel(x_ref, g_ref, w1_ref, w2_ref, gp_ref, o_ref):
    x = x_ref[...]
    h = _rms(x, g_ref[...]).astype(BF16)
    d_ff = w1_ref.shape[1]
    y = None
    for cidx in range(d_ff // D_MODEL):
        sl = slice(cidx * D_MODEL, (cidx + 1) * D_MODEL)
        a = jnp.maximum(jnp.dot(h, w1_ref[:, sl], preferred_element_type=F32), 0.0)
        part = jnp.dot((a * a).astype(BF16), w2_ref[sl, :], preferred_element_type=F32)
        y = part if y is None else y + part
    o_ref[...] = x + _rms(y, gp_ref[...])


def _mlp(x2, g, w1, w2, gp):
    t = x2.shape[0]
    tm = TM_MLP
    rows = pl.BlockSpec((tm, D_MODEL), lambda i: (i, 0))
    return pl.pallas_call(
        _mlp_kernel, grid=(t // tm,),
        in_specs=[rows, _const_spec((1, D_MODEL)), _const_spec(w1.shape), _const_spec(w2.shape),
                  _const_spec((1, D_MODEL))],
        out_specs=rows, out_shape=jax.ShapeDtypeStruct((t, D_MODEL), F32),
        compiler_params=pltpu.CompilerParams(dimension_semantics=("arbitrary",),
                                             vmem_limit_bytes=VMEM_LIMIT),
        name="mlp",
    )(x2, g, w1, w2, gp)


def _rope_tables(dim, seq):
    inv = 1.0 / (ROPE_THETA ** (jnp.arange(0, dim, 2, dtype=F32) / dim))
    ang = jnp.arange(seq, dtype=F32)[:, None] * inv[None, :]
    return jnp.cos(ang), jnp.sin(ang)


def _position_tables(seq):
    cos_b, sin_b = _rope_tables(B_ROT, seq)
    cos_d, sin_d = _rope_tables(D_ROPE, seq)
    one, zero = jnp.ones((seq, 1), F32), jnp.zeros((seq, 1), F32)
    cb = jnp.concatenate([jnp.tile(cos_b, (1, N_HEADS)), jnp.tile(one, (1, 96))], axis=1)
    sb = jnp.concatenate([jnp.tile(sin_b, (1, N_HEADS)), jnp.tile(zero, (1, 96))], axis=1)
    cq = jnp.concatenate([jnp.tile(one, (1, 64)), cos_d, cos_d, jnp.tile(zero, (1, 32))], axis=1)
    sq = jnp.concatenate([jnp.tile(zero, (1, 64)), sin_d, sin_d, jnp.tile(zero, (1, 32))], axis=1)
    ck = jnp.concatenate([jnp.tile(zero, (1, 64)), cos_d, cos_d, jnp.tile(zero, (1, 32))], axis=1)
    return jnp.concatenate([cb, sb, cq, sq, ck], axis=1)


def _pack_layer(w_in, d_w_uq, d_w_ukv, d_q_norm_g, d_kv_norm_g, a_b_s):
    perm = _moba_perm()
    z = lambda n: jnp.zeros((D_MODEL, n), w_in.dtype)
    a = w_in[:, 0:512]
    bq, bk, bv = w_in[:, 512:768], w_in[:, 768:1024], w_in[:, 1024:1280]
    c = w_in[:, 1280:2048]
    cq, ckv, kr = w_in[:, 2048:2240], w_in[:, 2240:2368], w_in[:, 2368:2400]
    kr_sw = jnp.concatenate([-kr[:, 16:32], kr[:, 0:16]], axis=1)
    w1 = jnp.concatenate([a, bq[:, perm], bk[:, perm], bv, c,
                          cq[:, 0:128], cq[:, 128:192], kr, z(32),
                          z(64), kr_sw, z(32), ckv], axis=1).astype(BF16)
    w_gate = w_in[:, 2400:].astype(BF16)

    nope, x1, x2 = d_w_uq[:, :, 0:64], d_w_uq[:, :, 64:80], d_w_uq[:, :, 80:96]
    zq = lambda n: jnp.zeros((D_Q_LORA, N_HEADS, n), d_w_uq.dtype)
    main = jnp.concatenate([nope, x1, x2, zq(32)], axis=2).reshape(D_Q_LORA, 512)
    swap = jnp.concatenate([zq(64), -x2, x1, zq(32)], axis=2).reshape(D_Q_LORA, 512)
    wq = jnp.concatenate([main, swap], axis=1)
    wq = jnp.concatenate([wq, jnp.zeros((256 - D_Q_LORA, 1024), wq.dtype)], axis=0).astype(BF16)
    gq = jnp.concatenate([d_q_norm_g, jnp.zeros((256 - D_Q_LORA,), F32)])[None, :]

    wk = jnp.concatenate([d_w_ukv[:, :, 0:64], jnp.zeros((D_KV_LORA, N_HEADS, 64), d_w_ukv.dtype)],
                         axis=2).reshape(D_KV_LORA, 512)
    wv = d_w_ukv[:, :, 64:128].reshape(D_KV_LORA, MIX)
    wkv = jnp.concatenate([wk, wv], axis=1).astype(BF16)
    gkv = d_kv_norm_g[None, :]
    bs = jnp.repeat(a_b_s.T, HEAD_DIM, axis=1)
    return w1, w_gate, wq, gq, wkv, gkv, bs


def kernel(x, g_pre_mix, w_in, a_ln_g, a_ln_b, a_w_s, a_b_s, c_w_conv, d_q_norm_g, d_w_uq,
           d_kv_norm_g, d_w_ukv, w_branch, w_out, g_post_mix, g_pre_mlp, w_mlp_in, w_mlp_out,
           g_post_mlp):
    batch, seq, _ = x.shape
    depth = w_in.shape[0]
    t = batch * seq
    tab = _position_tables(seq)
    x2 = x.reshape(t, D_MODEL)
    for l in range(depth):
        w1, w_gate, wq, gq, wkv, gkv, bs = _pack_layer(
            w_in[l], d_w_uq[l], d_w_ukv[l], d_q_norm_g[l], d_kv_norm_g[l], a_b_s[l])
        (auv, bqh, bql, bk, bvt, km, c3, dqt, dk, dvt) = _inproj(
            x2, g_pre_mix[l][None, :], w1, tab, gq, gkv, wq, wkv, seq)
        km = km.reshape(batch, seq // MOBA_BLOCK, MIX)
        out_b = _moba(bqh, bql, bk, bvt, km, batch, seq)
        out_d = _mla(dqt, dk, dvt, batch, seq)
        x2 = _merge(x2, auv, c3, out_b, out_d, g_pre_mix[l][None, :], w_gate,
                    a_ln_g[l][None, :], a_ln_b[l][None, :],
                    a_w_s[l].reshape(N_HEADS * A_CHUNK, A_CHUNK), bs, c_w_conv[l],
                    w_branch[l].astype(BF16), w_out[l].astype(BF16), g_post_mix[l][None, :], seq)
        x2 = _mlp(x2, g_pre_mlp[l][None, :], w_mlp_in[l].astype(BF16), w_mlp_out[l].astype(BF16),
                  g_post_mlp[l][None, :])
    return x2.reshape(batch, seq, D_MODEL)
```

```python
import functools
import math

import numpy as np
import jax
import jax.numpy as jnp
from jax import lax
from jax.experimental import pallas as pl
from jax.experimental.pallas import tpu as pltpu

F32 = jnp.float32
BF16 = jnp.bfloat16

D_MODEL = 1024
HEAD_DIM = 64
N_HEADS = 4
MIX = 256
A_CHUNK = 128
MOBA_BLOCK = 256
MOBA_TOPK = 3
B_ROT = 16
D_NOPE, D_ROPE, D_Q_LORA, D_KV_LORA = 64, 32, 192, 128
ROPE_THETA = 500000.0
NORM_EPS = 1e-6
LN_EPS = 1e-5
LOG2E = 1.4426950408889634

LANE = 128
TM_IN = 512
TM_MERGE = 256
TM_MLP = 512
TQ = 256
TK = 512
L_ROWS = 16
VMEM_LIMIT = 56 * 1024 * 1024

N1 = 2560
NEG_INF = float("-inf")


def _moba_perm():
    perm = np.zeros(MIX, np.int32)
    for h in range(N_HEADS):
        for j in range(8):
            perm[h * 8 + j] = h * HEAD_DIM + j
            perm[LANE + h * 8 + j] = h * HEAD_DIM + 8 + j
        for r in range(24):
            perm[32 + h * 24 + r] = h * HEAD_DIM + 16 + r
            perm[LANE + 32 + h * 24 + r] = h * HEAD_DIM + 40 + r
    return perm


def _moba_head_mask(idx, h):
    l = idx & (LANE - 1)
    return ((l >= 8 * h) & (l < 8 * h + 8)) | ((l >= 32 + 24 * h) & (l < 56 + 24 * h))


def _rms(x, g):
    return x * lax.rsqrt(jnp.mean(x * x, axis=-1, keepdims=True) + NORM_EPS) * g


def _gelu(x):
    return 0.5 * x * (1.0 + jnp.tanh(math.sqrt(2.0 / math.pi) * (x + 0.044715 * (x * x * x))))


def _const_spec(shape):
    nd = len(shape)
    return pl.BlockSpec(shape, lambda *_: (0,) * nd, pipeline_mode=pl.Buffered(1))


def _inproj_kernel(x_ref, g_ref, w1_ref, tab_ref, gq_ref, gkv_ref, wq_ref, wkv_ref,
                   auv_ref, bqh_ref, bql_ref, bk_ref, bvt_ref, km_ref, c3_ref,
                   dqt_ref, dk_ref, dvt_ref):
    tm = x_ref.shape[0]
    h = _rms(x_ref[...], g_ref[...]).astype(BF16)

    def proj(a, b):
        return jnp.dot(h, w1_ref[:, a:b], preferred_element_type=F32)

    auv_ref[...] = proj(0, 512)

    cb = tab_ref[:, 0:128]
    sb = tab_ref[:, 128:256]
    q0, q1 = proj(512, 640), proj(640, 768)
    q = jnp.concatenate([q0 * cb - q1 * sb, q1 * cb + q0 * sb], axis=1)
    k0, k1 = proj(768, 896), proj(896, 1024)
    k = jnp.concatenate([k0 * cb - k1 * sb, k1 * cb + k0 * sb], axis=1)
    v = proj(1024, 1280)
    qt = q.T
    qt_hi = qt.astype(BF16)
    qt_lo = (qt - qt_hi.astype(F32)).astype(BF16)
    vt = v.T.astype(BF16)
    bk_ref[...] = k.astype(BF16)
    for j in range(tm // TQ):
        bqh_ref[j] = qt_hi[:, j * TQ:(j + 1) * TQ]
        bql_ref[j] = qt_lo[:, j * TQ:(j + 1) * TQ]
    for j in range(tm // TK):
        bvt_ref[j] = vt[:, j * TK:(j + 1) * TK]
    for j in range(tm // MOBA_BLOCK):
        blk = k[j * MOBA_BLOCK:(j + 1) * MOBA_BLOCK, :]
        km_ref[0, j:j + 1, :] = jnp.sum(blk, axis=0, keepdims=True) * (1.0 / MOBA_BLOCK)

    c3_ref[...] = proj(1280, 2048)

    d = proj(2048, 2560)
    cq = d[:, 0:256]
    lane = lax.broadcasted_iota(jnp.int32, cq.shape, 1)
    ms = jnp.sum(jnp.where(lane < D_Q_LORA, cq * cq, 0.0), axis=-1, keepdims=True) * (1.0 / D_Q_LORA)
    cqn = (cq * lax.rsqrt(ms + NORM_EPS) * gq_ref[...]).astype(BF16)
    qq = jnp.dot(cqn, wq_ref[...], preferred_element_type=F32)
    cq_t = jnp.concatenate([tab_ref[:, 256:384]] * N_HEADS, axis=1)
    sq_t = jnp.concatenate([tab_ref[:, 384:512]] * N_HEADS, axis=1)
    qd = qq[:, 0:512] * cq_t + qq[:, 512:1024] * sq_t
    qdt = qd.T.astype(BF16)

    kvn = _rms(d[:, 384:512], gkv_ref[...]).astype(BF16)
    kv = jnp.dot(kvn, wkv_ref[...], preferred_element_type=F32)
    kr = d[:, 128:256] * tab_ref[:, 512:640] + d[:, 256:384] * tab_ref[:, 384:512]
    dk_ref[...] = (kv[:, 0:512] + jnp.concatenate([kr] * N_HEADS, axis=1)).astype(BF16)
    vdt = kv[:, 512:768].T.astype(BF16)
    for j in range(tm // TQ):
        dqt_ref[j] = qdt[:, j * TQ:(j + 1) * TQ]
    for j in range(tm // TK):
        dvt_ref[j] = vdt[:, j * TK:(j + 1) * TK]


def _inproj(x2, g, w1, tab, gq, gkv, wq, wkv, seq):
    t = x2.shape[0]
    tm = TM_IN
    nt = t // tm
    spt = seq // tm
    rows = lambda w: pl.BlockSpec((tm, w), lambda i: (i, 0))
    tblk = lambda r: pl.BlockSpec((tm // TQ, r, TQ), lambda i: (i, 0, 0))
    vblk = pl.BlockSpec((tm // TK, MIX, TK), lambda i: (i, 0, 0))
    out_shape = (
        jax.ShapeDtypeStruct((t, 512), F32),
        jax.ShapeDtypeStruct((t // TQ, MIX, TQ), BF16),
        jax.ShapeDtypeStruct((t // TQ, MIX, TQ), BF16),
        jax.ShapeDtypeStruct((t, MIX), BF16),
        jax.ShapeDtypeStruct((t // TK, MIX, TK), BF16),
        jax.ShapeDtypeStruct((nt, tm // MOBA_BLOCK, MIX), F32),
        jax.ShapeDtypeStruct((t, 768), F32),
        jax.ShapeDtypeStruct((t // TQ, 512, TQ), BF16),
        jax.ShapeDtypeStruct((t, 512), BF16),
        jax.ShapeDtypeStruct((t // TK, MIX, TK), BF16),
    )
    out_specs = (
        rows(512), tblk(MIX), tblk(MIX), rows(MIX), vblk,
        pl.BlockSpec((1, tm // MOBA_BLOCK, MIX), lambda i: (i, 0, 0)),
        rows(768), tblk(512), rows(512), vblk,
    )
    in_specs = [
        rows(D_MODEL), _const_spec((1, D_MODEL)), _const_spec((D_MODEL, N1)),
        pl.BlockSpec((tm, 640), lambda i: (i % spt, 0)),
        _const_spec((1, 256)), _const_spec((1, 128)),
        _const_spec((256, 1024)), _const_spec((128, 768)),
    ]
    return pl.pallas_call(
        _inproj_kernel, grid=(nt,), in_specs=in_specs, out_specs=out_specs, out_shape=out_shape,
        compiler_params=pltpu.CompilerParams(dimension_semantics=("arbitrary",),
                                             vmem_limit_bytes=VMEM_LIMIT),
        name="inproj",
    )(x2, g, w1, tab, gq, gkv, wq, wkv)


def _attend(scores, vt_ref, s_ref, pa_ref, pb_ref, acc_ref, i, c):
    hs = [slice(h * HEAD_DIM, (h + 1) * HEAD_DIM) for h in range(N_HEADS)]
    nw = lax.shift_right_logical(i, 1)
    row = lax.broadcasted_iota(jnp.int32, (TK, TQ), 0)
    col = lax.broadcasted_iota(jnp.int32, (TK, TQ), 1)
    keep = row <= col + (i & 1) * TQ

    m = []
    for h in range(N_HEADS):
        s = jnp.where(keep, scores(nw, h) * c, NEG_INF)
        s_ref[h, nw] = s
        m.append(jnp.max(s, axis=0, keepdims=True))

    def window(w, m):
        new = []
        for h in range(N_HEADS):
            s = scores(w, h) * c
            s_ref[h, w] = s
            new.append(jnp.maximum(m[h], jnp.max(s, axis=0, keepdims=True)))
        return tuple(new)

    m = lax.fori_loop(0, lax.shift_right_logical(nw, 1),
                      lambda j, m: window(2 * j + 1, window(2 * j, m)), tuple(m))
    m = lax.cond((nw & 1) == 1, lambda m: window(nw - 1, m), lambda m: m, m)

    acc_ref[...] = jnp.zeros_like(acc_ref)
    ones = jnp.ones((L_ROWS, TK), BF16)

    def probs(w, h, buf):
        buf[h] = jnp.exp2(s_ref[h, w] - m[h]).astype(BF16)

    def accumulate(w, h, buf):
        v_aug = jnp.concatenate([vt_ref[w, hs[h], :], ones], axis=0)
        acc_ref[h] += jnp.dot(v_aug, buf[h], preferred_element_type=F32)

    for h in range(N_HEADS):
        probs(0, h, pa_ref)

    def pass2(j, carry):
        w = 2 * j
        for h in range(N_HEADS):
            accumulate(w, h, pa_ref)
            probs(w + 1, h, pb_ref)
        for h in range(N_HEADS):
            accumulate(w + 1, h, pb_ref)
            probs(w + 2, h, pa_ref)
        return carry

    lax.fori_loop(0, lax.shift_right_logical(nw, 1), pass2, 0)
    for h in range(N_HEADS):
        accumulate(nw - (nw & 1), h, pa_ref)

    @pl.when((nw & 1) == 1)
    def _():
        for h in range(N_HEADS):
            probs(nw, h, pb_ref)
        for h in range(N_HEADS):
            accumulate(nw, h, pb_ref)
    out_t = jnp.concatenate(
        [acc_ref[h, 0:HEAD_DIM, :] * (1.0 / acc_ref[h, HEAD_DIM:HEAD_DIM + 1, :]) for h in range(N_HEADS)],
        axis=0)
    return out_t.T


def _moba_kernel(qh_ref, ql_ref, k_ref, vt_ref, km_ref, o_ref, bias_s, qm_s, s_ref, pa_ref, pb_ref, acc_ref):
    i = pl.program_id(1)
    c = (HEAD_DIM ** -0.5) * LOG2E
    qt_hi = qh_ref[0]
    qt_lo = ql_ref[0]

    km = km_ref[0]
    nb = km.shape[0]
    lane = lax.broadcasted_iota(jnp.int32, km.shape, 1)
    kmh = jnp.concatenate([jnp.where(_moba_head_mask(lane, h), km, 0.0) for h in range(N_HEADS)], axis=0)
    km_hi = kmh.astype(BF16)
    km_lo = (kmh - km_hi.astype(F32)).astype(BF16)
    gate = (jnp.dot(km_hi, qt_hi, preferred_element_type=F32)
            + jnp.dot(km_hi, qt_lo, preferred_element_type=F32)
            + jnp.dot(km_lo, qt_hi, preferred_element_type=F32))

    blk = lax.broadcasted_iota(jnp.int32, (nb, TQ), 0)
    blkf = blk.astype(F32)
    for h in range(N_HEADS):
        g = jnp.where(blk < i, gate[h * nb:(h + 1) * nb, :], NEG_INF)
        bias = jnp.full((nb, TQ), NEG_INF, F32)
        for _ in range(MOBA_TOPK):
            mx = jnp.max(g, axis=0, keepdims=True)
            first = jnp.min(jnp.where(g == mx, blkf, float(nb)), axis=0, keepdims=True)
            hit = blkf == first
            bias = jnp.where(hit & (mx > NEG_INF), 0.0, bias)
            g = jnp.where(hit, NEG_INF, g)
        bias_s[h * nb:(h + 1) * nb, :] = jnp.where(blk == i, 0.0, bias)

    row = lax.broadcasted_iota(jnp.int32, (MIX, TQ), 0)
    for h in range(N_HEADS):
        qm_s[h] = jnp.where(_moba_head_mask(row, h), qt_hi, jnp.zeros_like(qt_hi))

    def scores(w, h):
        kw = k_ref[pl.ds(pl.multiple_of(w * TK, TK), TK), :]
        s = jnp.dot(kw, qm_s[h], preferred_element_type=F32)
        b0 = bias_s[pl.ds(h * nb + 2 * w, 1), :]
        b1 = bias_s[pl.ds(h * nb + 2 * w + 1, 1), :]
        return jnp.concatenate([s[0:MOBA_BLOCK] + b0, s[MOBA_BLOCK:TK] + b1], axis=0)

    o_ref[...] = _attend(scores, vt_ref, s_ref, pa_ref, pb_ref, acc_ref, i, c).astype(o_ref.dtype)


def _moba(qh, ql, k, vt, km, batch, seq):
    nq = seq // TQ
    t = batch * seq
    qspec = pl.BlockSpec((1, MIX, TQ), lambda b, i: (b * nq + i, 0, 0))
    return pl.pallas_call(
        _moba_kernel, grid=(batch, nq),
        in_specs=[qspec, qspec,
                  pl.BlockSpec((seq, MIX), lambda b, i: (b, 0)),
                  pl.BlockSpec((seq // TK, MIX, TK), lambda b, i: (b, 0, 0)),
                  pl.BlockSpec((1, seq // MOBA_BLOCK, MIX), lambda b, i: (b, 0, 0))],
        out_specs=pl.BlockSpec((TQ, MIX), lambda b, i: (b * nq + i, 0)),
        out_shape=jax.ShapeDtypeStruct((t, MIX), BF16),
        scratch_shapes=[pltpu.VMEM((N_HEADS * (seq // MOBA_BLOCK), TQ), F32),
                        pltpu.VMEM((N_HEADS, MIX, TQ), BF16),
                        pltpu.VMEM((N_HEADS, seq // TK, TK, TQ), F32),
                        pltpu.VMEM((N_HEADS, TK, TQ), BF16),
                        pltpu.VMEM((N_HEADS, TK, TQ), BF16),
                        pltpu.VMEM((N_HEADS, HEAD_DIM + L_ROWS, TQ), F32)],
        compiler_params=pltpu.CompilerParams(dimension_semantics=("arbitrary", "arbitrary"),
                                             vmem_limit_bytes=VMEM_LIMIT),
        name="moba",
    )(qh, ql, k, vt, km)


def _mla_kernel(qt_ref, k_ref, vt_ref, o_ref, s_ref, pa_ref, pb_ref, acc_ref):
    i = pl.program_id(1)
    c = ((D_NOPE + D_ROPE) ** -0.5) * LOG2E

    def scores(w, h):
        kw = k_ref[pl.ds(pl.multiple_of(w * TK, TK), TK), h * LANE:(h + 1) * LANE]
        return jnp.dot(kw, qt_ref[0, h * LANE:(h + 1) * LANE, :], preferred_element_type=F32)

    o_ref[...] = _attend(scores, vt_ref, s_ref, pa_ref, pb_ref, acc_ref, i, c).astype(o_ref.dtype)


def _mla(qt, k, vt, batch, seq):
    nq = seq // TQ
    t = batch * seq
    return pl.pallas_call(
        _mla_kernel, grid=(batch, nq),
        in_specs=[pl.BlockSpec((1, 512, TQ), lambda b, i: (b * nq + i, 0, 0)),
                  pl.BlockSpec((seq, 512), lambda b, i: (b, 0)),
                  pl.BlockSpec((seq // TK, MIX, TK), lambda b, i: (b, 0, 0))],
        out_specs=pl.BlockSpec((TQ, MIX), lambda b, i: (b * nq + i, 0)),
        out_shape=jax.ShapeDtypeStruct((t, MIX), BF16),
        scratch_shapes=[pltpu.VMEM((N_HEADS, seq // TK, TK, TQ), F32),
                        pltpu.VMEM((N_HEADS, TK, TQ), BF16),
                        pltpu.VMEM((N_HEADS, TK, TQ), BF16),
                        pltpu.VMEM((N_HEADS, HEAD_DIM + L_ROWS, TQ), F32)],
        compiler_params=pltpu.CompilerParams(dimension_semantics=("arbitrary", "arbitrary"),
                                             vmem_limit_bytes=VMEM_LIMIT),
        name="mla",
    )(qt, k, vt)


def _merge_kernel(x_ref, auv_ref, c3_ref, c3p_ref, ob_ref, od_ref,
                  g_ref, wg_ref, lng_ref, lnb_ref, ws_ref, bs_ref, wc_ref, wb_ref, wo_ref, gp_ref,
                  o_ref, *, steps_per_seq):
    tm = x_ref.shape[0]
    x = x_ref[...]
    h = _rms(x, g_ref[...]).astype(BF16)

    u = _gelu(auv_ref[:, 0:MIX])
    v = _gelu(auv_ref[:, MIX:2 * MIX])
    mu = jnp.mean(v, axis=-1, keepdims=True)
    vc = v - mu
    var = jnp.mean(vc * vc, axis=-1, keepdims=True)
    v = (vc * lax.rsqrt(var + LN_EPS) * lng_ref[...] + lnb_ref[...]).astype(BF16)
    wrow = lax.broadcasted_iota(jnp.int32, ws_ref.shape, 0) & (A_CHUNK - 1)
    wcol = lax.broadcasted_iota(jnp.int32, ws_ref.shape, 1)
    ws = jnp.where(wrow >= wcol, ws_ref[...], 0.0).astype(BF16)
    lane = lax.broadcasted_iota(jnp.int32, (A_CHUNK, MIX), 1)
    svs = []
    for cidx in range(tm // A_CHUNK):
        r = jnp.dot(ws, v[cidx * A_CHUNK:(cidx + 1) * A_CHUNK, :], preferred_element_type=F32)
        sv = jnp.where(lane < 64, r[0:128],
                       jnp.where(lane < 128, r[128:256], jnp.where(lane < 192, r[256:384], r[384:512])))
        svs.append(sv + bs_ref[...])
    out_a = (u * jnp.concatenate(svs, axis=0)).astype(BF16)

    z = c3_ref[:, MIX:2 * MIX] * c3_ref[:, 2 * MIX:3 * MIX]
    zp = c3p_ref[:, MIX:2 * MIX] * c3p_ref[:, 2 * MIX:3 * MIX]
    zp = jnp.where(pl.program_id(0) % steps_per_seq == 0, 0.0, zp)
    zm1, zm2 = zp[7:8, :], zp[6:7, :]
    rowi = lax.broadcasted_iota(jnp.int32, z.shape, 0)
    z1 = jnp.where(rowi == 0, zm1, pltpu.roll(z, 1, axis=0))
    z2 = jnp.where(rowi == 0, zm2, jnp.where(rowi == 1, zm1, pltpu.roll(z, 2, axis=0)))
    conv = wc_ref[0:1, :] * z2 + wc_ref[1:2, :] * z1 + wc_ref[2:3, :] * z
    out_c = (c3_ref[:, 0:MIX] * conv).astype(BF16)

    branches = (out_a, ob_ref[...], out_c, od_ref[...])
    merged = None
    for bi in range(4):
        logits = jnp.dot(h, wg_ref[:, bi * D_MODEL:(bi + 1) * D_MODEL], preferred_element_type=F32)
        gate = 1.0 / (1.0 + jnp.exp(-logits))
        term = gate * jnp.dot(branches[bi], wb_ref[bi], preferred_element_type=F32)
        merged = term if merged is None else merged + term
    y = jnp.dot(merged.astype(BF16), wo_ref[...], preferred_element_type=F32)
    o_ref[...] = x + _rms(y, gp_ref[...])


def _merge(x2, auv, c3, ob, od, g, wg, lng, lnb, ws, bs, wc, wb, wo, gp, seq):
    t = x2.shape[0]
    tm = TM_MERGE
    rows = lambda w: pl.BlockSpec((tm, w), lambda i: (i, 0))
    prev8 = pl.BlockSpec((8, 768), lambda i: (jnp.maximum(i * (tm // 8) - 1, 0), 0))
    in_specs = [
        rows(D_MODEL), rows(512), rows(768), prev8, rows(MIX), rows(MIX),
        _const_spec((1, D_MODEL)), _const_spec((D_MODEL, 4 * D_MODEL)),
        _const_spec((1, MIX)), _const_spec((1, MIX)),
        _const_spec((N_HEADS * A_CHUNK, A_CHUNK)), _const_spec((A_CHUNK, MIX)),
        _const_spec((3, MIX)), _const_spec((4, MIX, D_MODEL)),
        _const_spec((D_MODEL, D_MODEL)), _const_spec((1, D_MODEL)),
    ]
    return pl.pallas_call(
        functools.partial(_merge_kernel, steps_per_seq=seq // tm), grid=(t // tm,),
        in_specs=in_specs, out_specs=rows(D_MODEL),
        out_shape=jax.ShapeDtypeStruct((t, D_MODEL), F32),
        compiler_params=pltpu.CompilerParams(dimension_semantics=("arbitrary",),
                                             vmem_limit_bytes=VMEM_LIMIT),
        name="merge",
    )(x2, auv, c3, c3, ob, od, g, wg, lng, lnb, ws, bs, wc, wb, wo, gp)


def _mlp_kernel(x_ref, g_ref, w1_ref, w2_ref, gp_ref, o_ref):
    x = x_ref[...]
    h = _rms(x, g_ref[...]).astype(BF16)
    d_ff = w1_ref.shape[1]
    y = None
    for cidx in range(d_ff // D_MODEL):
        sl = slice(cidx * D_MODEL, (cidx + 1) * D_MODEL)
        a = jnp.maximum(jnp.dot(h, w1_ref[:, sl], preferred_element_type=F32), 0.0)
        part = jnp.dot((a * a).astype(BF16), w2_ref[sl, :], preferred_element_type=F32)
        y = part if y is None else y + part
    o_ref[...] = x + _rms(y, gp_ref[...])


def _mlp(x2, g, w1, w2, gp):
    t = x2.shape[0]
    tm = TM_MLP
    rows = pl.BlockSpec((tm, D_MODEL), lambda i: (i, 0))
    return pl.pallas_call(
        _mlp_kernel, grid=(t // tm,),
        in_specs=[rows, _const_spec((1, D_MODEL)), _const_spec(w1.shape), _const_spec(w2.shape),
                  _const_spec((1, D_MODEL))],
        out_specs=rows, out_shape=jax.ShapeDtypeStruct((t, D_MODEL), F32),
        compiler_params=pltpu.CompilerParams(dimension_semantics=("arbitrary",),
                                             vmem_limit_bytes=VMEM_LIMIT),
        name="mlp",
    )(x2, g, w1, w2, gp)


def _rope_tables(dim, seq):
    inv = 1.0 / (ROPE_THETA ** (jnp.arange(0, dim, 2, dtype=F32) / dim))
    ang = jnp.arange(seq, dtype=F32)[:, None] * inv[None, :]
    return jnp.cos(ang), jnp.sin(ang)


def _position_tables(seq):
    cos_b, sin_b = _rope_tables(B_ROT, seq)
    cos_d, sin_d = _rope_tables(D_ROPE, seq)
    one, zero = jnp.ones((seq, 1), F32), jnp.zeros((seq, 1), F32)
    cb = jnp.concatenate([jnp.tile(cos_b, (1, N_HEADS)), jnp.tile(one, (1, 96))], axis=1)
    sb = jnp.concatenate([jnp.tile(sin_b, (1, N_HEADS)), jnp.tile(zero, (1, 96))], axis=1)
    cq = jnp.concatenate([jnp.tile(one, (1, 64)), cos_d, cos_d, jnp.tile(zero, (1, 32))], axis=1)
    sq = jnp.concatenate([jnp.tile(zero, (1, 64)), sin_d, sin_d, jnp.tile(zero, (1, 32))], axis=1)
    ck = jnp.concatenate([jnp.tile(zero, (1, 64)), cos_d, cos_d, jnp.tile(zero, (1, 32))], axis=1)
    return jnp.concatenate([cb, sb, cq, sq, ck], axis=1)


def _pack_layer(w_in, d_w_uq, d_w_ukv, d_q_norm_g, d_kv_norm_g, a_b_s):
    perm = _moba_perm()
    z = lambda n: jnp.zeros((D_MODEL, n), w_in.dtype)
    a = w_in[:, 0:512]
    bq, bk, bv = w_in[:, 512:768], w_in[:, 768:1024], w_in[:, 1024:1280]
    c = w_in[:, 1280:2048]
    cq, ckv, kr = w_in[:, 2048:2240], w_in[:, 2240:2368], w_in[:, 2368:2400]
    kr_sw = jnp.concatenate([-kr[:, 16:32], kr[:, 0:16]], axis=1)
    w1 = jnp.concatenate([a, bq[:, perm], bk[:, perm], bv, c,
                          cq[:, 0:128], cq[:, 128:192], kr, z(32),
                          z(64), kr_sw, z(32), ckv], axis=1).astype(BF16)
    w_gate = w_in[:, 2400:].astype(BF16)

    nope, x1, x2 = d_w_uq[:, :, 0:64], d_w_uq[:, :, 64:80], d_w_uq[:, :, 80:96]
    zq = lambda n: jnp.zeros((D_Q_LORA, N_HEADS, n), d_w_uq.dtype)
    main = jnp.concatenate([nope, x1, x2, zq(32)], axis=2).reshape(D_Q_LORA, 512)
    swap = jnp.concatenate([zq(64), -x2, x1, zq(32)], axis=2).reshape(D_Q_LORA, 512)
    wq = jnp.concatenate([main, swap], axis=1)
    wq = jnp.concatenate([wq, jnp.zeros((256 - D_Q_LORA, 1024), wq.dtype)], axis=0).astype(BF16)
    gq = jnp.concatenate([d_q_norm_g, jnp.zeros((256 - D_Q_LORA,), F32)])[None, :]

    wk = jnp.concatenate([d_w_ukv[:, :, 0:64], jnp.zeros((D_KV_LORA, N_HEADS, 64), d_w_ukv.dtype)],
                         axis=2).reshape(D_KV_LORA, 512)
    wv = d_w_ukv[:, :, 64:128].reshape(D_KV_LORA, MIX)
    wkv = jnp.concatenate([wk, wv], axis=1).astype(BF16)
    gkv = d_kv_norm_g[None, :]
    bs = jnp.repeat(a_b_s.T, HEAD_DIM, axis=1)
    return w1, w_gate, wq, gq, wkv, gkv, bs


def kernel(x, g_pre_mix, w_in, a_ln_g, a_ln_b, a_w_s, a_b_s, c_w_conv, d_q_norm_g, d_w_uq,
           d_kv_norm_g, d_w_ukv, w_branch, w_out, g_post_mix, g_pre_mlp, w_mlp_in, w_mlp_out,
           g_post_mlp):
    batch, seq, _ = x.shape
    depth = w_in.shape[0]
    t = batch * seq
    tab = _position_tables(seq)
    x2 = x.reshape(t, D_MODEL)
    for l in range(depth):
        w1, w_gate, wq, gq, wkv, gkv, bs = _pack_layer(
            w_in[l], d_w_uq[l], d_w_ukv[l], d_q_norm_g[l], d_kv_norm_g[l], a_b_s[l])
        (auv, bqh, bql, bk, bvt, km, c3, dqt, dk, dvt) = _inproj(
            x2, g_pre_mix[l][None, :], w1, tab, gq, gkv, wq, wkv, seq)
        km = km.reshape(batch, seq // MOBA_BLOCK, MIX)
        out_b = _moba(bqh, bql, bk, bvt, km, batch, seq)
        out_d = _mla(dqt, dk, dvt, batch, seq)
        x2 = _merge(x2, auv, c3, out_b, out_d, g_pre_mix[l][None, :], w_gate,
                    a_ln_g[l][None, :], a_ln_b[l][None, :],
                    a_w_s[l].reshape(N_HEADS * A_CHUNK, A_CHUNK), bs, c_w_conv[l],
                    w_branch[l].astype(BF16), w_out[l].astype(BF16), g_post_mix[l][None, :], seq)
        x2 = _mlp(x2, g_pre_mlp[l][None, :], w_mlp_in[l].astype(BF16), w_mlp_out[l].astype(BF16),
                  g_post_mlp[l][None, :])
    return x2.reshape(batch, seq, D_MODEL)
```

```python
import functools
import math

import numpy as np
import jax
import jax.numpy as jnp
from jax import lax
from jax.experimental import pallas as pl
from jax.experimental.pallas import tpu as pltpu

F32 = jnp.float32
BF16 = jnp.bfloat16

D_MODEL = 1024
HEAD_DIM = 64
N_HEADS = 4
MIX = 256
A_CHUNK = 128
MOBA_BLOCK = 256
MOBA_TOPK = 3
B_ROT = 16
D_NOPE, D_ROPE, D_Q_LORA, D_KV_LORA = 64, 32, 192, 128
ROPE_THETA = 500000.0
NORM_EPS = 1e-6
LN_EPS = 1e-5
LOG2E = 1.4426950408889634

LANE = 128
TM_IN = 512
TM_MERGE = 512
SUB_MERGE = 256
TM_MLP = 1024
SUB_MLP = 512
TQ = 256
TK = 512
L_ROWS = 16
VMEM_LIMIT = 56 * 1024 * 1024

N1 = 2560
NEG_INF = float("-inf")


def _moba_perm():
    perm = np.zeros(MIX, np.int32)
    for h in range(N_HEADS):
        for j in range(8):
            perm[h * 8 + j] = h * HEAD_DIM + j
            perm[LANE + h * 8 + j] = h * HEAD_DIM + 8 + j
        for r in range(24):
            perm[32 + h * 24 + r] = h * HEAD_DIM + 16 + r
            perm[LANE + 32 + h * 24 + r] = h * HEAD_DIM + 40 + r
    return perm


def _moba_head_mask(idx, h):
    l = idx & (LANE - 1)
    return ((l >= 8 * h) & (l < 8 * h + 8)) | ((l >= 32 + 24 * h) & (l < 56 + 24 * h))


def _rms(x, g):
    return x * lax.rsqrt(jnp.mean(x * x, axis=-1, keepdims=True) + NORM_EPS) * g


def _gelu(x):
    return 0.5 * x * (1.0 + jnp.tanh(math.sqrt(2.0 / math.pi) * (x + 0.044715 * (x * x * x))))


def _const_spec(shape):
    nd = len(shape)
    return pl.BlockSpec(shape, lambda *_: (0,) * nd, pipeline_mode=pl.Buffered(1))


def _inproj_kernel(x_ref, g_ref, w1_ref, tab_ref, gq_ref, gkv_ref, wq_ref, wkv_ref,
                   auv_ref, bqh_ref, bql_ref, bk_ref, bvt_ref, km_ref, c3_ref,
                   dqt_ref, dk_ref, dvt_ref):
    tm = x_ref.shape[0]
    h = _rms(x_ref[...], g_ref[...]).astype(BF16)

    def proj(a, b):
        return jnp.dot(h, w1_ref[:, a:b], preferred_element_type=F32)

    auv_ref[...] = proj(0, 512)

    cb = tab_ref[:, 0:128]
    sb = tab_ref[:, 128:256]
    q0, q1 = proj(512, 640), proj(640, 768)
    q = jnp.concatenate([q0 * cb - q1 * sb, q1 * cb + q0 * sb], axis=1)
    k0, k1 = proj(768, 896), proj(896, 1024)
    k = jnp.concatenate([k0 * cb - k1 * sb, k1 * cb + k0 * sb], axis=1)
    v = proj(1024, 1280)
    qt = q.T
    qt_hi = qt.astype(BF16)
    qt_lo = (qt - qt_hi.astype(F32)).astype(BF16)
    vt = v.T.astype(BF16)
    bk_ref[...] = k.astype(BF16)
    for j in range(tm // TQ):
        bqh_ref[j] = qt_hi[:, j * TQ:(j + 1) * TQ]
        bql_ref[j] = qt_lo[:, j * TQ:(j + 1) * TQ]
    for j in range(tm // TK):
        bvt_ref[j] = vt[:, j * TK:(j + 1) * TK]
    for j in range(tm // MOBA_BLOCK):
        blk = k[j * MOBA_BLOCK:(j + 1) * MOBA_BLOCK, :]
        km_ref[0, j:j + 1, :] = jnp.sum(blk, axis=0, keepdims=True) * (1.0 / MOBA_BLOCK)

    c3_ref[...] = proj(1280, 2048)

    d = proj(2048, 2560)
    cq = d[:, 0:256]
    lane = lax.broadcasted_iota(jnp.int32, cq.shape, 1)
    ms = jnp.sum(jnp.where(lane < D_Q_LORA, cq * cq, 0.0), axis=-1, keepdims=True) * (1.0 / D_Q_LORA)
    cqn = (cq * lax.rsqrt(ms + NORM_EPS) * gq_ref[...]).astype(BF16)
    qq = jnp.dot(cqn, wq_ref[...], preferred_element_type=F32)
    cq_t = jnp.concatenate([tab_ref[:, 256:384]] * N_HEADS, axis=1)
    sq_t = jnp.concatenate([tab_ref[:, 384:512]] * N_HEADS, axis=1)
    qd = qq[:, 0:512] * cq_t + qq[:, 512:1024] * sq_t
    qdt = qd.T.astype(BF16)

    kvn = _rms(d[:, 384:512], gkv_ref[...]).astype(BF16)
    kv = jnp.dot(kvn, wkv_ref[...], preferred_element_type=F32)
    kr = d[:, 128:256] * tab_ref[:, 512:640] + d[:, 256:384] * tab_ref[:, 384:512]
    dk_ref[...] = (kv[:, 0:512] + jnp.concatenate([kr] * N_HEADS, axis=1)).astype(BF16)
    vdt = kv[:, 512:768].T.astype(BF16)
    for j in range(tm // TQ):
        dqt_ref[j] = qdt[:, j * TQ:(j + 1) * TQ]
    for j in range(tm // TK):
        dvt_ref[j] = vdt[:, j * TK:(j + 1) * TK]


def _inproj(x2, g, w1, tab, gq, gkv, wq, wkv, seq):
    t = x2.shape[0]
    tm = TM_IN
    nt = t // tm
    spt = seq // tm
    rows = lambda w: pl.BlockSpec((tm, w), lambda i: (i, 0))
    tblk = lambda r: pl.BlockSpec((tm // TQ, r, TQ), lambda i: (i, 0, 0))
    vblk = pl.BlockSpec((tm // TK, MIX, TK), lambda i: (i, 0, 0))
    out_shape = (
        jax.ShapeDtypeStruct((t, 512), F32),
        jax.ShapeDtypeStruct((t // TQ, MIX, TQ), BF16),
        jax.ShapeDtypeStruct((t // TQ, MIX, TQ), BF16),
        jax.ShapeDtypeStruct((t, MIX), BF16),
        jax.ShapeDtypeStruct((t // TK, MIX, TK), BF16),
        jax.ShapeDtypeStruct((nt, tm // MOBA_BLOCK, MIX), F32),
        jax.ShapeDtypeStruct((t, 768), F32),
        jax.ShapeDtypeStruct((t // TQ, 512, TQ), BF16),
        jax.ShapeDtypeStruct((t, 512), BF16),
        jax.ShapeDtypeStruct((t // TK, MIX, TK), BF16),
    )
    out_specs = (
        rows(512), tblk(MIX), tblk(MIX), rows(MIX), vblk,
        pl.BlockSpec((1, tm // MOBA_BLOCK, MIX), lambda i: (i, 0, 0)),
        rows(768), tblk(512), rows(512), vblk,
    )
    in_specs = [
        rows(D_MODEL), _const_spec((1, D_MODEL)), _const_spec((D_MODEL, N1)),
        pl.BlockSpec((tm, 640), lambda i: (i % spt, 0)),
        _const_spec((1, 256)), _const_spec((1, 128)),
        _const_spec((256, 1024)), _const_spec((128, 768)),
    ]
    return pl.pallas_call(
        _inproj_kernel, grid=(nt,), in_specs=in_specs, out_specs=out_specs, out_shape=out_shape,
        compiler_params=pltpu.CompilerParams(dimension_semantics=("arbitrary",),
                                             vmem_limit_bytes=VMEM_LIMIT),
        name="inproj",
    )(x2, g, w1, tab, gq, gkv, wq, wkv)


def _attend(scores, vt_ref, s_ref, o_ref, i, c):
    hs = [slice(h * HEAD_DIM, (h + 1) * HEAD_DIM) for h in range(N_HEADS)]
    nw = lax.shift_right_logical(i, 1)
    row = lax.broadcasted_iota(jnp.int32, (TK, TQ), 0)
    col = lax.broadcasted_iota(jnp.int32, (TK, TQ), 1)
    keep = row <= col + (i & 1) * TQ
    ones = jnp.ones((L_ROWS, TK), BF16)

    def body(k):
        def pass1(h):
            m = None
            for w in range(k + 1):
                s = scores(w, h) * c
                if w == k:
                    s = jnp.where(keep, s, NEG_INF)
                s_ref[h, w] = s
                mw = jnp.max(s, axis=0, keepdims=True)
                m = mw if m is None else jnp.maximum(m, mw)
            return m

        def pass2(h, m):
            acc = None
            for w in range(k + 1):
                p = jnp.exp2(s_ref[h, w] - m).astype(BF16)
                v_aug = jnp.concatenate([vt_ref[w, hs[h], :], ones], axis=0)
                r = jnp.dot(v_aug, p, preferred_element_type=F32)
                acc = r if acc is None else acc + r
            return acc[0:HEAD_DIM, :] * (1.0 / acc[HEAD_DIM:HEAD_DIM + 1, :])

        m = [pass1(0)]
        out = []
        for h in range(1, N_HEADS):
            m.append(pass1(h))
            out.append(pass2(h - 1, m[h - 1]))
        out.append(pass2(N_HEADS - 1, m[N_HEADS - 1]))
        o_ref[...] = jnp.concatenate(out, axis=0).T.astype(o_ref.dtype)

    for k in range(s_ref.shape[1]):
        pl.when(nw == k)(functools.partial(body, k))


def _moba_kernel(qh_ref, ql_ref, k_ref, vt_ref, km_ref, o_ref, bias_s, qm_s, s_ref):
    i = pl.program_id(1)
    c = (HEAD_DIM ** -0.5) * LOG2E
    qt_hi = qh_ref[0]
    qt_lo = ql_ref[0]

    km = km_ref[0]
    nb = km.shape[0]
    lane = lax.broadcasted_iota(jnp.int32, km.shape, 1)
    kmh = jnp.concatenate([jnp.where(_moba_head_mask(lane, h), km, 0.0) for h in range(N_HEADS)], axis=0)
    km_hi = kmh.astype(BF16)
    km_lo = (kmh - km_hi.astype(F32)).astype(BF16)
    gate = (jnp.dot(km_hi, qt_hi, preferred_element_type=F32)
            + jnp.dot(km_hi, qt_lo, preferred_element_type=F32)
            + jnp.dot(km_lo, qt_hi, preferred_element_type=F32))

    blk = lax.broadcasted_iota(jnp.int32, (nb, TQ), 0)
    blkf = blk.astype(F32)
    for h in range(N_HEADS):
        g = jnp.where(blk < i, gate[h * nb:(h + 1) * nb, :], NEG_INF)
        bias = jnp.full((nb, TQ), NEG_INF, F32)
        for _ in range(MOBA_TOPK):
            mx = jnp.max(g, axis=0, keepdims=True)
            first = jnp.min(jnp.where(g == mx, blkf, float(nb)), axis=0, keepdims=True)
            hit = blkf == first
            bias = jnp.where(hit & (mx > NEG_INF), 0.0, bias)
            g = jnp.where(hit, NEG_INF, g)
        bias_s[h * nb:(h + 1) * nb, :] = jnp.where(blk == i, 0.0, bias)

    row = lax.broadcasted_iota(jnp.int32, (MIX, TQ), 0)
    for h in range(N_HEADS):
        qm_s[h] = jnp.where(_moba_head_mask(row, h), qt_hi, jnp.zeros_like(qt_hi))

    def scores(w, h):
        s = jnp.dot(k_ref[w * TK:(w + 1) * TK, :], qm_s[h], preferred_element_type=F32)
        b0 = bias_s[h * nb + 2 * w:h * nb + 2 * w + 1, :]
        b1 = bias_s[h * nb + 2 * w + 1:h * nb + 2 * w + 2, :]
        return jnp.concatenate([s[0:MOBA_BLOCK] + b0, s[MOBA_BLOCK:TK] + b1], axis=0)

    _attend(scores, vt_ref, s_ref, o_ref, i, c)


def _moba(qh, ql, k, vt, km, batch, seq):
    nq = seq // TQ
    t = batch * seq
    qspec = pl.BlockSpec((1, MIX, TQ), lambda b, i: (b * nq + i, 0, 0))
    return pl.pallas_call(
        _moba_kernel, grid=(batch, nq),
        in_specs=[qspec, qspec,
                  pl.BlockSpec((seq, MIX), lambda b, i: (b, 0)),
                  pl.BlockSpec((seq // TK, MIX, TK), lambda b, i: (b, 0, 0)),
                  pl.BlockSpec((1, seq // MOBA_BLOCK, MIX), lambda b, i: (b, 0, 0))],
        out_specs=pl.BlockSpec((TQ, MIX), lambda b, i: (b * nq + i, 0)),
        out_shape=jax.ShapeDtypeStruct((t, MIX), BF16),
        scratch_shapes=[pltpu.VMEM((N_HEADS * (seq // MOBA_BLOCK), TQ), F32),
                        pltpu.VMEM((N_HEADS, MIX, TQ), BF16),
                        pltpu.VMEM((N_HEADS, seq // TK, TK, TQ), F32)],
        compiler_params=pltpu.CompilerParams(dimension_semantics=("arbitrary", "arbitrary"),
                                             vmem_limit_bytes=VMEM_LIMIT),
        name="moba",
    )(qh, ql, k, vt, km)


def _mla_kernel(qt_ref, k_ref, vt_ref, o_ref, s_ref):
    i = pl.program_id(1)
    c = ((D_NOPE + D_ROPE) ** -0.5) * LOG2E

    def scores(w, h):
        kw = k_ref[w * TK:(w + 1) * TK, h * LANE:(h + 1) * LANE]
        return jnp.dot(kw, qt_ref[0, h * LANE:(h + 1) * LANE, :], preferred_element_type=F32)

    _attend(scores, vt_ref, s_ref, o_ref, i, c)


def _mla(qt, k, vt, batch, seq):
    nq = seq // TQ
    t = batch * seq
    return pl.pallas_call(
        _mla_kernel, grid=(batch, nq),
        in_specs=[pl.BlockSpec((1, 512, TQ), lambda b, i: (b * nq + i, 0, 0)),
                  pl.BlockSpec((seq, 512), lambda b, i: (b, 0)),
                  pl.BlockSpec((seq // TK, MIX, TK), lambda b, i: (b, 0, 0))],
        out_specs=pl.BlockSpec((TQ, MIX), lambda b, i: (b * nq + i, 0)),
        out_shape=jax.ShapeDtypeStruct((t, MIX), BF16),
        scratch_shapes=[pltpu.VMEM((N_HEADS, seq // TK, TK, TQ), F32)],
        compiler_params=pltpu.CompilerParams(dimension_semantics=("arbitrary", "arbitrary"),
                                             vmem_limit_bytes=VMEM_LIMIT),
        name="mla",
    )(qt, k, vt)


def _merge_rows(r0, first_in_seq, x_ref, auv_ref, c3_ref, c3p_ref, ob_ref, od_ref,
                g_ref, wg_ref, lng_ref, lnb_ref, ws, bs_ref, wc_ref, wb_ref, wo_ref, gp_ref, o_ref):
    rows = slice(r0, r0 + SUB_MERGE)
    y_b = jnp.dot(ob_ref[rows, :], wb_ref[1], preferred_element_type=F32)
    y_d = jnp.dot(od_ref[rows, :], wb_ref[3], preferred_element_type=F32)
    x = x_ref[rows, :]
    h = _rms(x, g_ref[...]).astype(BF16)

    u = _gelu(auv_ref[rows, 0:MIX])
    v = _gelu(auv_ref[rows, MIX:2 * MIX])
    mu = jnp.mean(v, axis=-1, keepdims=True)
    vc = v - mu
    var = jnp.mean(vc * vc, axis=-1, keepdims=True)
    v = (vc * lax.rsqrt(var + LN_EPS) * lng_ref[...] + lnb_ref[...]).astype(BF16)
    lane = lax.broadcasted_iota(jnp.int32, (A_CHUNK, MIX), 1)
    svs = []
    for cidx in range(SUB_MERGE // A_CHUNK):
        r = jnp.dot(ws, v[cidx * A_CHUNK:(cidx + 1) * A_CHUNK, :], preferred_element_type=F32)
        sv = jnp.where(lane < 64, r[0:128],
                       jnp.where(lane < 128, r[128:256], jnp.where(lane < 192, r[256:384], r[384:512])))
        svs.append(sv + bs_ref[...])
    out_a = (u * jnp.concatenate(svs, axis=0)).astype(BF16)

    z = c3_ref[rows, MIX:2 * MIX] * c3_ref[rows, 2 * MIX:3 * MIX]
    if r0 == 0:
        zp = c3p_ref[:, MIX:2 * MIX] * c3p_ref[:, 2 * MIX:3 * MIX]
        zp = jnp.where(first_in_seq, 0.0, zp)
    else:
        zp = c3_ref[r0 - 8:r0, MIX:2 * MIX] * c3_ref[r0 - 8:r0, 2 * MIX:3 * MIX]
    zm1, zm2 = zp[7:8, :], zp[6:7, :]
    rowi = lax.broadcasted_iota(jnp.int32, z.shape, 0)
    z1 = jnp.where(rowi == 0, zm1, pltpu.roll(z, 1, axis=0))
    z2 = jnp.where(rowi == 0, zm2, jnp.where(rowi == 1, zm1, pltpu.roll(z, 2, axis=0)))
    conv = wc_ref[0:1, :] * z2 + wc_ref[1:2, :] * z1 + wc_ref[2:3, :] * z
    out_c = (c3_ref[rows, 0:MIX] * conv).astype(BF16)

    def gate(bi):
        logits = jnp.dot(h, wg_ref[:, bi * D_MODEL:(bi + 1) * D_MODEL], preferred_element_type=F32)
        return 1.0 / (1.0 + jnp.exp(-logits))

    merged = gate(1) * y_b + gate(3) * y_d
    merged = merged + gate(0) * jnp.dot(out_a, wb_ref[0], preferred_element_type=F32)
    merged = merged + gate(2) * jnp.dot(out_c, wb_ref[2], preferred_element_type=F32)
    y = jnp.dot(merged.astype(BF16), wo_ref[...], preferred_element_type=F32)
    o_ref[rows, :] = x + _rms(y, gp_ref[...])


def _merge_kernel(x_ref, auv_ref, c3_ref, c3p_ref, ob_ref, od_ref,
                  g_ref, wg_ref, lng_ref, lnb_ref, ws_ref, bs_ref, wc_ref, wb_ref, wo_ref, gp_ref,
                  o_ref, *, steps_per_seq):
    wrow = lax.broadcasted_iota(jnp.int32, ws_ref.shape, 0) & (A_CHUNK - 1)
    wcol = lax.broadcasted_iota(jnp.int32, ws_ref.shape, 1)
    ws = jnp.where(wrow >= wcol, ws_ref[...], 0.0).astype(BF16)
    first_in_seq = pl.program_id(0) % steps_per_seq == 0
    for r0 in range(0, x_ref.shape[0], SUB_MERGE):
        _merge_rows(r0, first_in_seq, x_ref, auv_ref, c3_ref, c3p_ref, ob_ref, od_ref,
                    g_ref, wg_ref, lng_ref, lnb_ref, ws, bs_ref, wc_ref, wb_ref, wo_ref, gp_ref, o_ref)


def _merge(x2, auv, c3, ob, od, g, wg, lng, lnb, ws, bs, wc, wb, wo, gp, seq):
    t = x2.shape[0]
    tm = TM_MERGE
    rows = lambda w: pl.BlockSpec((tm, w), lambda i: (i, 0))
    prev8 = pl.BlockSpec((8, 768), lambda i: (jnp.maximum(i * (tm // 8) - 1, 0), 0))
    in_specs = [
        rows(D_MODEL), rows(512), rows(768), prev8, rows(MIX), rows(MIX),
        _const_spec((1, D_MODEL)), _const_spec((D_MODEL, 4 * D_MODEL)),
        _const_spec((1, MIX)), _const_spec((1, MIX)),
        _const_spec((N_HEADS * A_CHUNK, A_CHUNK)), _const_spec((A_CHUNK, MIX)),
        _const_spec((3, MIX)), _const_spec((4, MIX, D_MODEL)),
        _const_spec((D_MODEL, D_MODEL)), _const_spec((1, D_MODEL)),
    ]
    return pl.pallas_call(
        functools.partial(_merge_kernel, steps_per_seq=seq // tm), grid=(t // tm,),
        in_specs=in_specs, out_specs=rows(D_MODEL),
        out_shape=jax.ShapeDtypeStruct((t, D_MODEL), F32),
        compiler_params=pltpu.CompilerParams(dimension_semantics=("arbitrary",),
                                             vmem_limit_bytes=VMEM_LIMIT),
        name="merge",
    )(x2, auv, c3, c3, ob, od, g, wg, lng, lnb, ws, bs, wc, wb, wo, gp)


def _mlp_kernel(x_ref, g_ref, w1_ref, w2_ref, gp_ref, o_ref):
    d_ff = w1_ref.shape[1]
    for r0 in range(0, x_ref.shape[0], SUB_MLP):
        rows = slice(r0, r0 + SUB_MLP)
        x = x_ref[rows, :]
        h = _rms(x, g_ref[...]).astype(BF16)
        y = None
        for cidx in range(d_ff // D_MODEL):
            sl = slice(cidx * D_MODEL, (cidx + 1) * D_MODEL)
            a = jnp.maximum(jnp.dot(h, w1_ref[:, sl], preferred_element_type=F32), 0.0)
            part = jnp.dot((a * a).astype(BF16), w2_ref[sl, :], preferred_element_type=F32)
            y = part if y is None else y + part
        o_ref[rows, :] = x + _rms(y, gp_ref[...])


def _mlp(x2, g, w1, w2, gp):
    t = x2.shape[0]
    tm = TM_MLP
    rows = pl.BlockSpec((tm, D_MODEL), lambda i: (i, 0))
    return pl.pallas_call(
        _mlp_kernel, grid=(t // tm,),
        in_specs=[rows, _const_spec((1, D_MODEL)), _const_spec(w1.shape), _const_spec(w2.shape),
                  _const_spec((1, D_MODEL))],
        out_specs=rows, out_shape=jax.ShapeDtypeStruct((t, D_MODEL), F32),
        compiler_params=pltpu.CompilerParams(dimension_semantics=("arbitrary",),
                                             vmem_limit_bytes=VMEM_LIMIT),
        name="mlp",
    )(x2, g, w1, w2, gp)


def _rope_tables(dim, seq):
    inv = 1.0 / (ROPE_THETA ** (jnp.arange(0, dim, 2, dtype=F32) / dim))
    ang = jnp.arange(seq, dtype=F32)[:, None] * inv[None, :]
    return jnp.cos(ang), jnp.sin(ang)


def _position_tables(seq):
    cos_b, sin_b = _rope_tables(B_ROT, seq)
    cos_d, sin_d = _rope_tables(D_ROPE, seq)
    one, zero = jnp.ones((seq, 1), F32), jnp.zeros((seq, 1), F32)
    cb = jnp.concatenate([jnp.tile(cos_b, (1, N_HEADS)), jnp.tile(one, (1, 96))], axis=1)
    sb = jnp.concatenate([jnp.tile(sin_b, (1, N_HEADS)), jnp.tile(zero, (1, 96))], axis=1)
    cq = jnp.concatenate([jnp.tile(one, (1, 64)), cos_d, cos_d, jnp.tile(zero, (1, 32))], axis=1)
    sq = jnp.concatenate([jnp.tile(zero, (1, 64)), sin_d, sin_d, jnp.tile(zero, (1, 32))], axis=1)
    ck = jnp.concatenate([jnp.tile(zero, (1, 64)), cos_d, cos_d, jnp.tile(zero, (1, 32))], axis=1)
    return jnp.concatenate([cb, sb, cq, sq, ck], axis=1)


def _pack_layer(w_in, d_w_uq, d_w_ukv, d_q_norm_g, d_kv_norm_g, a_b_s):
    perm = _moba_perm()
    z = lambda n: jnp.zeros((D_MODEL, n), w_in.dtype)
    a = w_in[:, 0:512]
    bq, bk, bv = w_in[:, 512:768], w_in[:, 768:1024], w_in[:, 1024:1280]
    c = w_in[:, 1280:2048]
    cq, ckv, kr = w_in[:, 2048:2240], w_in[:, 2240:2368], w_in[:, 2368:2400]
    kr_sw = jnp.concatenate([-kr[:, 16:32], kr[:, 0:16]], axis=1)
    w1 = jnp.concatenate([a, bq[:, perm], bk[:, perm], bv, c,
                          cq[:, 0:128], cq[:, 128:192], kr, z(32),
                          z(64), kr_sw, z(32), ckv], axis=1).astype(BF16)
    w_gate = w_in[:, 2400:].astype(BF16)

    nope, x1, x2 = d_w_uq[:, :, 0:64], d_w_uq[:, :, 64:80], d_w_uq[:, :, 80:96]
    zq = lambda n: jnp.zeros((D_Q_LORA, N_HEADS, n), d_w_uq.dtype)
    main = jnp.concatenate([nope, x1, x2, zq(32)], axis=2).reshape(D_Q_LORA, 512)
    swap = jnp.concatenate([zq(64), -x2, x1, zq(32)], axis=2).reshape(D_Q_LORA, 512)
    wq = jnp.concatenate([main, swap], axis=1)
    wq = jnp.concatenate([wq, jnp.zeros((256 - D_Q_LORA, 1024), wq.dtype)], axis=0).astype(BF16)
    gq = jnp.concatenate([d_q_norm_g, jnp.zeros((256 - D_Q_LORA,), F32)])[None, :]

    wk = jnp.concatenate([d_w_ukv[:, :, 0:64], jnp.zeros((D_KV_LORA, N_HEADS, 64), d_w_ukv.dtype)],
                         axis=2).reshape(D_KV_LORA, 512)
    wv = d_w_ukv[:, :, 64:128].reshape(D_KV_LORA, MIX)
    wkv = jnp.concatenate([wk, wv], axis=1).astype(BF16)
    gkv = d_kv_norm_g[None, :]
    bs = jnp.repeat(a_b_s.T, HEAD_DIM, axis=1)
    return w1, w_gate, wq, gq, wkv, gkv, bs


def kernel(x, g_pre_mix, w_in, a_ln_g, a_ln_b, a_w_s, a_b_s, c_w_conv, d_q_norm_g, d_w_uq,
           d_kv_norm_g, d_w_ukv, w_branch, w_out, g_post_mix, g_pre_mlp, w_mlp_in, w_mlp_out,
           g_post_mlp):
    batch, seq, _ = x.shape
    depth = w_in.shape[0]
    t = batch * seq
    tab = _position_tables(seq)
    x2 = x.reshape(t, D_MODEL)
    for l in range(depth):
        w1, w_gate, wq, gq, wkv, gkv, bs = _pack_layer(
            w_in[l], d_w_uq[l], d_w_ukv[l], d_q_norm_g[l], d_kv_norm_g[l], a_b_s[l])
        (auv, bqh, bql, bk, bvt, km, c3, dqt, dk, dvt) = _inproj(
            x2, g_pre_mix[l][None, :], w1, tab, gq, gkv, wq, wkv, seq)
        km = km.reshape(batch, seq // MOBA_BLOCK, MIX)
        out_b = _moba(bqh, bql, bk, bvt, km, batch, seq)
        out_d = _mla(dqt, dk, dvt, batch, seq)
        x2 = _merge(x2, auv, c3, out_b, out_d, g_pre_mix[l][None, :], w_gate,
                    a_ln_g[l][None, :], a_ln_b[l][None, :],
                    a_w_s[l].reshape(N_HEADS * A_CHUNK, A_CHUNK), bs, c_w_conv[l],
                    w_branch[l].astype(BF16), w_out[l].astype(BF16), g_post_mix[l][None, :], seq)
        x2 = _mlp(x2, g_pre_mlp[l][None, :], w_mlp_in[l].astype(BF16), w_mlp_out[l].astype(BF16),
                  g_post_mlp[l][None, :])
    return x2.reshape(batch, seq, D_MODEL)
```

```python
import functools
import math

import jax
import jax.numpy as jnp
from jax import lax
from jax.experimental import pallas as pl
from jax.experimental.pallas import tpu as pltpu

F32 = jnp.float32
BF16 = jnp.bfloat16

D_MODEL = 1024
HEAD_DIM = 64
N_HEADS = 4
MIX = 256
A_CHUNK = 128
MOBA_BLOCK = 256
MOBA_TOPK = 3
B_ROT = 16
D_NOPE, D_ROPE, D_Q_LORA, D_KV_LORA = 64, 32, 192, 128
ROPE_THETA = 500000.0
NORM_EPS = 1e-6
LN_EPS = 1e-5
LOG2E = 1.4426950408889634
MOBA_QSCALE = (HEAD_DIM ** -0.5) * LOG2E
MLA_QSCALE = ((D_NOPE + D_ROPE) ** -0.5) * LOG2E

LANE = 128
TM_IN = 512
TM_MERGE = 512
SUB_MERGE = 256
TM_MLP = 1024
SUB_MLP = 512
TQ = 256
TK = 512
L_ROWS = 16
VMEM_LIMIT = 56 * 1024 * 1024

N1 = 2560
NEG_INF = float("-inf")


def _moba_cols(w):
    lead = w.shape[:-1]
    w4 = w.reshape(lead + (N_HEADS, HEAD_DIM))
    part = lambda a, b: w4[..., a:b].reshape(lead + (N_HEADS * (b - a),))
    return jnp.concatenate([part(0, 8), part(16, 40), part(8, 16), part(40, 64)], axis=-1)


def _moba_head_mask(idx, h):
    l = idx & (LANE - 1)
    return ((l >= 8 * h) & (l < 8 * h + 8)) | ((l >= 32 + 24 * h) & (l < 56 + 24 * h))


def _rms(x, g):
    return x * lax.rsqrt(jnp.mean(x * x, axis=-1, keepdims=True) + NORM_EPS) * g


def _gelu(x):
    return 0.5 * x * (1.0 + jnp.tanh(math.sqrt(2.0 / math.pi) * (x + 0.044715 * (x * x * x))))


def _layer_spec(arr, l):
    nd = arr.ndim - 1
    return pl.BlockSpec((None,) + arr.shape[1:], lambda *_: (l,) + (0,) * nd, pipeline_mode=pl.Buffered(1))


def _inproj_kernel(x_ref, tab_ref, g_ref, w1_ref, gq_ref, gkv_ref, wq_ref, wkv_ref,
                   auv_ref, bqh_ref, bql_ref, bk_ref, bvt_ref, km_ref, c3_ref,
                   dqt_ref, dk_ref, dvt_ref):
    tm = x_ref.shape[0]
    h = _rms(x_ref[...], g_ref[...]).astype(BF16)

    def proj(a, b):
        return jnp.dot(h, w1_ref[:, a:b], preferred_element_type=F32)

    auv_ref[...] = proj(0, 512)

    cb = tab_ref[:, 0:128]
    sb = tab_ref[:, 128:256]

    def rope(y):
        y0, y1 = y[:, 0:LANE], y[:, LANE:2 * LANE]
        return jnp.concatenate([y0 * cb - y1 * sb, y1 * cb + y0 * sb], axis=1)

    q = rope(proj(512, 768)) * MOBA_QSCALE
    k = rope(proj(768, 1024))
    v = proj(1024, 1280)
    qt = q.T
    qt_hi = qt.astype(BF16)
    qt_lo = (qt - qt_hi.astype(F32)).astype(BF16)
    vt = v.T.astype(BF16)
    bk_ref[...] = k.astype(BF16)
    for j in range(tm // TQ):
        bqh_ref[j] = qt_hi[:, j * TQ:(j + 1) * TQ]
        bql_ref[j] = qt_lo[:, j * TQ:(j + 1) * TQ]
    for j in range(tm // TK):
        bvt_ref[j] = vt[:, j * TK:(j + 1) * TK]
    for j in range(tm // MOBA_BLOCK):
        blk = k[j * MOBA_BLOCK:(j + 1) * MOBA_BLOCK, :]
        km_ref[0, j:j + 1, :] = jnp.sum(blk, axis=0, keepdims=True) * (1.0 / MOBA_BLOCK)

    c3_ref[...] = proj(1280, 2048)

    d = proj(2048, 2560)
    cq = d[:, 0:256]
    lane = lax.broadcasted_iota(jnp.int32, cq.shape, 1)
    ms = jnp.sum(jnp.where(lane < D_Q_LORA, cq * cq, 0.0), axis=-1, keepdims=True) * (1.0 / D_Q_LORA)
    cqn = (cq * lax.rsqrt(ms + NORM_EPS) * gq_ref[...]).astype(BF16)
    qq = jnp.dot(cqn, wq_ref[...], preferred_element_type=F32)
    cq_t = jnp.concatenate([tab_ref[:, 256:384]] * N_HEADS, axis=1)
    sq_t = jnp.concatenate([tab_ref[:, 384:512]] * N_HEADS, axis=1)
    qd = (qq[:, 0:512] * cq_t + qq[:, 512:1024] * sq_t) * MLA_QSCALE
    qdt = qd.T.astype(BF16)

    kvn = _rms(d[:, 384:512], gkv_ref[...]).astype(BF16)
    kv = jnp.dot(kvn, wkv_ref[...], preferred_element_type=F32)
    kr = d[:, 128:256] * tab_ref[:, 512:640] + d[:, 256:384] * tab_ref[:, 384:512]
    dk_ref[...] = (kv[:, 0:512] + jnp.concatenate([kr] * N_HEADS, axis=1)).astype(BF16)
    vdt = kv[:, 512:768].T.astype(BF16)
    for j in range(tm // TQ):
        dqt_ref[j] = qdt[:, j * TQ:(j + 1) * TQ]
    for j in range(tm // TK):
        dvt_ref[j] = vdt[:, j * TK:(j + 1) * TK]


def _inproj(x2, l, p, tab, seq):
    t = x2.shape[0]
    tm = TM_IN
    nt = t // tm
    spt = seq // tm
    rows = lambda w: pl.BlockSpec((tm, w), lambda i: (i, 0))
    tblk = lambda r: pl.BlockSpec((tm // TQ, r, TQ), lambda i: (i, 0, 0))
    vblk = pl.BlockSpec((tm // TK, MIX, TK), lambda i: (i, 0, 0))
    out_shape = (
        jax.ShapeDtypeStruct((t, 512), F32),
        jax.ShapeDtypeStruct((t // TQ, MIX, TQ), BF16),
        jax.ShapeDtypeStruct((t // TQ, MIX, TQ), BF16),
        jax.ShapeDtypeStruct((t, MIX), BF16),
        jax.ShapeDtypeStruct((t // TK, MIX, TK), BF16),
        jax.ShapeDtypeStruct((nt, tm // MOBA_BLOCK, MIX), F32),
        jax.ShapeDtypeStruct((t, 768), F32),
        jax.ShapeDtypeStruct((t // TQ, 512, TQ), BF16),
        jax.ShapeDtypeStruct((t, 512), BF16),
        jax.ShapeDtypeStruct((t // TK, MIX, TK), BF16),
    )
    out_specs = (
        rows(512), tblk(MIX), tblk(MIX), rows(MIX), vblk,
        pl.BlockSpec((1, tm // MOBA_BLOCK, MIX), lambda i: (i, 0, 0)),
        rows(768), tblk(512), rows(512), vblk,
    )
    params = (p["g_pre_mix"], p["w1"], p["gq"], p["gkv"], p["wq"], p["wkv"])
    in_specs = [rows(D_MODEL), pl.BlockSpec((tm, 640), lambda i: (i % spt, 0))]
    in_specs += [_layer_spec(a, l) for a in params]
    return pl.pallas_call(
        _inproj_kernel, grid=(nt,), in_specs=in_specs, out_specs=out_specs, out_shape=out_shape,
        compiler_params=pltpu.CompilerParams(dimension_semantics=("arbitrary",),
                                             vmem_limit_bytes=VMEM_LIMIT),
        name="inproj",
    )(x2, tab, *params)


def _attend(make_scores, vt_ref, s_ref, o_ref, i):
    hs = [slice(h * HEAD_DIM, (h + 1) * HEAD_DIM) for h in range(N_HEADS)]
    nw = lax.shift_right_logical(i, 1)
    row = lax.broadcasted_iota(jnp.int32, (TK, TQ), 0)
    col = lax.broadcasted_iota(jnp.int32, (TK, TQ), 1)
    keep = row <= col + (i & 1) * TQ
    ones = jnp.ones((L_ROWS, TK), BF16)

    def body(k):
        scores = make_scores()

        def pass1(h):
            m = None
            for w in range(k + 1):
                s = scores(w, h)
                if w == k:
                    s = jnp.where(keep, s, NEG_INF)
                s_ref[h, w] = s
                mw = jnp.max(s, axis=0, keepdims=True)
                m = mw if m is None else jnp.maximum(m, mw)
            return m

        def pass2(h, m):
            acc = None
            for w in range(k + 1):
                p = jnp.exp2(s_ref[h, w] - m).astype(BF16)
                v_aug = jnp.concatenate([vt_ref[w, hs[h], :], ones], axis=0)
                r = jnp.dot(v_aug, p, preferred_element_type=F32)
                acc = r if acc is None else acc + r
            return acc[0:HEAD_DIM, :] * (1.0 / acc[HEAD_DIM:HEAD_DIM + 1, :])

        m = [pass1(0)]
        out = []
        for h in range(1, N_HEADS):
            m.append(pass1(h))
            out.append(pass2(h - 1, m[h - 1]))
        out.append(pass2(N_HEADS - 1, m[N_HEADS - 1]))
        o_ref[...] = jnp.concatenate(out, axis=0).T.astype(o_ref.dtype)

    for k in range(s_ref.shape[1]):
        pl.when(nw == k)(functools.partial(body, k))


def _moba_kernel(qh_ref, ql_ref, k_ref, vt_ref, km_ref, o_ref, qm_s, s_ref):
    i = pl.program_id(1)
    qt_hi = qh_ref[0]
    row = lax.broadcasted_iota(jnp.int32, (MIX, TQ), 0)
    for h in range(N_HEADS):
        qm_s[h] = jnp.where(_moba_head_mask(row, h), qt_hi, jnp.zeros_like(qt_hi))

    def make_scores():
        km = km_ref[0]
        nb = km.shape[0]
        lane = lax.broadcasted_iota(jnp.int32, km.shape, 1)
        kmh = jnp.concatenate([jnp.where(_moba_head_mask(lane, h), km, 0.0) for h in range(N_HEADS)], axis=0)
        km_hi = kmh.astype(BF16)
        km_lo = (kmh - km_hi.astype(F32)).astype(BF16)
        gate = (jnp.dot(km_hi, qt_hi, preferred_element_type=F32)
                + jnp.dot(km_hi, ql_ref[0], preferred_element_type=F32)
                + jnp.dot(km_lo, qt_hi, preferred_element_type=F32))

        blk = lax.broadcasted_iota(jnp.int32, (nb, TQ), 0)
        blkf = blk.astype(F32)
        biases = []
        for h in range(N_HEADS):
            g = jnp.where(blk < i, gate[h * nb:(h + 1) * nb, :], NEG_INF)
            bias = jnp.full((nb, TQ), NEG_INF, F32)
            for _ in range(MOBA_TOPK):
                mx = jnp.max(g, axis=0, keepdims=True)
                first = jnp.min(jnp.where(g == mx, blkf, float(nb)), axis=0, keepdims=True)
                hit = blkf == first
                bias = jnp.where(hit & (mx > NEG_INF), 0.0, bias)
                g = jnp.where(hit, NEG_INF, g)
            biases.append(jnp.where(blk == i, 0.0, bias))

        def scores(w, h):
            s = jnp.dot(k_ref[w * TK:(w + 1) * TK, :], qm_s[h], preferred_element_type=F32)
            b = biases[h]
            return jnp.concatenate([s[0:MOBA_BLOCK] + b[2 * w:2 * w + 1, :],
                                    s[MOBA_BLOCK:TK] + b[2 * w + 1:2 * w + 2, :]], axis=0)

        return scores

    _attend(make_scores, vt_ref, s_ref, o_ref, i)


def _moba(qh, ql, k, vt, km, batch, seq):
    nq = seq // TQ
    t = batch * seq
    qspec = pl.BlockSpec((1, MIX, TQ), lambda b, i: (b * nq + i, 0, 0))
    return pl.pallas_call(
        _moba_kernel, grid=(batch, nq),
        in_specs=[qspec, qspec,
                  pl.BlockSpec((seq, MIX), lambda b, i: (b, 0)),
                  pl.BlockSpec((seq // TK, MIX, TK), lambda b, i: (b, 0, 0)),
                  pl.BlockSpec((1, seq // MOBA_BLOCK, MIX), lambda b, i: (b, 0, 0))],
        out_specs=pl.BlockSpec((TQ, MIX), lambda b, i: (b * nq + i, 0)),
        out_shape=jax.ShapeDtypeStruct((t, MIX), BF16),
        scratch_shapes=[pltpu.VMEM((N_HEADS, MIX, TQ), BF16),
                        pltpu.VMEM((N_HEADS, seq // TK, TK, TQ), F32)],
        compiler_params=pltpu.CompilerParams(dimension_semantics=("arbitrary", "arbitrary"),
                                             vmem_limit_bytes=VMEM_LIMIT),
        name="moba",
    )(qh, ql, k, vt, km)


def _mla_kernel(qt_ref, k_ref, vt_ref, o_ref, s_ref):
    i = pl.program_id(1)

    def scores(w, h):
        kw = k_ref[w * TK:(w + 1) * TK, h * LANE:(h + 1) * LANE]
        return jnp.dot(kw, qt_ref[0, h * LANE:(h + 1) * LANE, :], preferred_element_type=F32)

    _attend(lambda: scores, vt_ref, s_ref, o_ref, i)


def _mla(qt, k, vt, batch, seq):
    nq = seq // TQ
    t = batch * seq
    return pl.pallas_call(
        _mla_kernel, grid=(batch, nq),
        in_specs=[pl.BlockSpec((1, 512, TQ), lambda b, i: (b * nq + i, 0, 0)),
                  pl.BlockSpec((seq, 512), lambda b, i: (b, 0)),
                  pl.BlockSpec((seq // TK, MIX, TK), lambda b, i: (b, 0, 0))],
        out_specs=pl.BlockSpec((TQ, MIX), lambda b, i: (b * nq + i, 0)),
        out_shape=jax.ShapeDtypeStruct((t, MIX), BF16),
        scratch_shapes=[pltpu.VMEM((N_HEADS, seq // TK, TK, TQ), F32)],
        compiler_params=pltpu.CompilerParams(dimension_semantics=("arbitrary", "arbitrary"),
                                             vmem_limit_bytes=VMEM_LIMIT),
        name="mla",
    )(qt, k, vt)


def _merge_rows(r0, first_in_seq, x_ref, auv_ref, c3_ref, c3p_ref, ob_ref, od_ref,
                g_ref, wg_ref, lng_ref, lnb_ref, ws, bs_ref, wc_ref, wb_ref, wo_ref, gp_ref, o_ref):
    rows = slice(r0, r0 + SUB_MERGE)
    y_b = jnp.dot(ob_ref[rows, :], wb_ref[1], preferred_element_type=F32)
    y_d = jnp.dot(od_ref[rows, :], wb_ref[3], preferred_element_type=F32)
    x = x_ref[rows, :]
    h = _rms(x, g_ref[...]).astype(BF16)

    u = _gelu(auv_ref[rows, 0:MIX])
    v = _gelu(auv_ref[rows, MIX:2 * MIX])
    mu = jnp.mean(v, axis=-1, keepdims=True)
    vc = v - mu
    var = jnp.mean(vc * vc, axis=-1, keepdims=True)
    v = (vc * lax.rsqrt(var + LN_EPS) * lng_ref[...] + lnb_ref[...]).astype(BF16)
    lane = lax.broadcasted_iota(jnp.int32, (A_CHUNK, MIX), 1)
    svs = []
    for cidx in range(SUB_MERGE // A_CHUNK):
        r = jnp.dot(ws, v[cidx * A_CHUNK:(cidx + 1) * A_CHUNK, :], preferred_element_type=F32)
        sv = jnp.where(lane < 64, r[0:128],
                       jnp.where(lane < 128, r[128:256], jnp.where(lane < 192, r[256:384], r[384:512])))
        svs.append(sv + bs_ref[...])
    out_a = (u * jnp.concatenate(svs, axis=0)).astype(BF16)

    z = c3_ref[rows, MIX:2 * MIX] * c3_ref[rows, 2 * MIX:3 * MIX]
    if r0 == 0:
        zp = c3p_ref[:, MIX:2 * MIX] * c3p_ref[:, 2 * MIX:3 * MIX]
        zp = jnp.where(first_in_seq, 0.0, zp)
    else:
        zp = c3_ref[r0 - 8:r0, MIX:2 * MIX] * c3_ref[r0 - 8:r0, 2 * MIX:3 * MIX]
    zm1, zm2 = zp[7:8, :], zp[6:7, :]
    rowi = lax.broadcasted_iota(jnp.int32, z.shape, 0)
    z1 = jnp.where(rowi == 0, zm1, pltpu.roll(z, 1, axis=0))
    z2 = jnp.where(rowi == 0, zm2, jnp.where(rowi == 1, zm1, pltpu.roll(z, 2, axis=0)))
    conv = wc_ref[0:1, :] * z2 + wc_ref[1:2, :] * z1 + wc_ref[2:3, :] * z
    out_c = (c3_ref[rows, 0:MIX] * conv).astype(BF16)

    def gate(bi):
        logits = jnp.dot(h, wg_ref[:, bi * D_MODEL:(bi + 1) * D_MODEL], preferred_element_type=F32)
        return 1.0 / (1.0 + jnp.exp(-logits))

    merged = gate(1) * y_b + gate(3) * y_d
    merged = merged + gate(0) * jnp.dot(out_a, wb_ref[0], preferred_element_type=F32)
    merged = merged + gate(2) * jnp.dot(out_c, wb_ref[2], preferred_element_type=F32)
    y = jnp.dot(merged.astype(BF16), wo_ref[...], preferred_element_type=F32)
    o_ref[rows, :] = x + _rms(y, gp_ref[...])


def _merge_kernel(x_ref, auv_ref, c3_ref, c3p_ref, ob_ref, od_ref,
                  g_ref, wg_ref, lng_ref, lnb_ref, ws_ref, bs_ref, wc_ref, wb_ref, wo_ref, gp_ref,
                  o_ref, *, steps_per_seq):
    wrow = lax.broadcasted_iota(jnp.int32, ws_ref.shape, 0) & (A_CHUNK - 1)
    wcol = lax.broadcasted_iota(jnp.int32, ws_ref.shape, 1)
    ws = jnp.where(wrow >= wcol, ws_ref[...], 0.0).astype(BF16)
    first_in_seq = pl.program_id(0) % steps_per_seq == 0
    for r0 in range(0, x_ref.shape[0], SUB_MERGE):
        _merge_rows(r0, first_in_seq, x_ref, auv_ref, c3_ref, c3p_ref, ob_ref, od_ref,
                    g_ref, wg_ref, lng_ref, lnb_ref, ws, bs_ref, wc_ref, wb_ref, wo_ref, gp_ref, o_ref)


def _merge(x2, auv, c3, ob, od, l, p, seq):
    t = x2.shape[0]
    tm = TM_MERGE
    rows = lambda w: pl.BlockSpec((tm, w), lambda i: (i, 0))
    prev8 = pl.BlockSpec((8, 768), lambda i: (jnp.maximum(i * (tm // 8) - 1, 0), 0))
    params = (p["g_pre_mix"], p["w_gate"], p["a_ln_g"], p["a_ln_b"], p["a_w_s"], p["a_b_s"],
              p["c_w_conv"], p["w_branch"], p["w_out"], p["g_post_mix"])
    in_specs = [rows(D_MODEL), rows(512), rows(768), prev8, rows(MIX), rows(MIX)]
    in_specs += [_layer_spec(a, l) for a in params]
    return pl.pallas_call(
        functools.partial(_merge_kernel, steps_per_seq=seq // tm), grid=(t // tm,),
        in_specs=in_specs, out_specs=rows(D_MODEL),
        out_shape=jax.ShapeDtypeStruct((t, D_MODEL), F32),
        compiler_params=pltpu.CompilerParams(dimension_semantics=("arbitrary",),
                                             vmem_limit_bytes=VMEM_LIMIT),
        name="merge",
    )(x2, auv, c3, c3, ob, od, *params)


def _mlp_kernel(x_ref, g_ref, w1_ref, w2_ref, gp_ref, o_ref):
    d_ff = w1_ref.shape[1]
    for r0 in range(0, x_ref.shape[0], SUB_MLP):
        rows = slice(r0, r0 + SUB_MLP)
        x = x_ref[rows, :]
        h = _rms(x, g_ref[...]).astype(BF16)
        y = None
        for cidx in range(d_ff // D_MODEL):
            sl = slice(cidx * D_MODEL, (cidx + 1) * D_MODEL)
            a = jnp.maximum(jnp.dot(h, w1_ref[:, sl], preferred_element_type=F32), 0.0)
            part = jnp.dot((a * a).astype(BF16), w2_ref[sl, :], preferred_element_type=F32)
            y = part if y is None else y + part
        o_ref[rows, :] = x + _rms(y, gp_ref[...])


def _mlp(x2, l, p):
    t = x2.shape[0]
    tm = TM_MLP
    rows = pl.BlockSpec((tm, D_MODEL), lambda i: (i, 0))
    params = (p["g_pre_mlp"], p["w_mlp_in"], p["w_mlp_out"], p["g_post_mlp"])
    return pl.pallas_call(
        _mlp_kernel, grid=(t // tm,),
        in_specs=[rows] + [_layer_spec(a, l) for a in params],
        out_specs=rows, out_shape=jax.ShapeDtypeStruct((t, D_MODEL), F32),
        compiler_params=pltpu.CompilerParams(dimension_semantics=("arbitrary",),
                                             vmem_limit_bytes=VMEM_LIMIT),
        name="mlp",
    )(x2, *params)


def _rope_tables(dim, seq):
    inv = 1.0 / (ROPE_THETA ** (jnp.arange(0, dim, 2, dtype=F32) / dim))
    ang = jnp.arange(seq, dtype=F32)[:, None] * inv[None, :]
    return jnp.cos(ang), jnp.sin(ang)


def _position_tables(seq):
    cos_b, sin_b = _rope_tables(B_ROT, seq)
    cos_d, sin_d = _rope_tables(D_ROPE, seq)
    one, zero = jnp.ones((seq, 1), F32), jnp.zeros((seq, 1), F32)
    cb = jnp.concatenate([jnp.tile(cos_b, (1, N_HEADS)), jnp.tile(one, (1, 96))], axis=1)
    sb = jnp.concatenate([jnp.tile(sin_b, (1, N_HEADS)), jnp.tile(zero, (1, 96))], axis=1)
    cq = jnp.concatenate([jnp.tile(one, (1, 64)), cos_d, cos_d, jnp.tile(zero, (1, 32))], axis=1)
    sq = jnp.concatenate([jnp.tile(zero, (1, 64)), sin_d, sin_d, jnp.tile(zero, (1, 32))], axis=1)
    ck = jnp.concatenate([jnp.tile(zero, (1, 64)), cos_d, cos_d, jnp.tile(zero, (1, 32))], axis=1)
    return jnp.concatenate([cb, sb, cq, sq, ck], axis=1)


def _pack_params(g_pre_mix, w_in, a_ln_g, a_ln_b, a_w_s, a_b_s, c_w_conv, d_q_norm_g, d_w_uq, d_kv_norm_g,
                 d_w_ukv, w_branch, w_out, g_post_mix, g_pre_mlp, w_mlp_in, w_mlp_out, g_post_mlp):
    depth = w_in.shape[0]
    z = lambda n: jnp.zeros((depth, D_MODEL, n), w_in.dtype)
    a = w_in[..., 0:512]
    bq, bk, bv = w_in[..., 512:768], w_in[..., 768:1024], w_in[..., 1024:1280]
    c = w_in[..., 1280:2048]
    cq, ckv, kr = w_in[..., 2048:2240], w_in[..., 2240:2368], w_in[..., 2368:2400]
    kr_sw = jnp.concatenate([-kr[..., 16:32], kr[..., 0:16]], axis=-1)
    w1 = jnp.concatenate([a, _moba_cols(bq), _moba_cols(bk), bv, c,
                          cq[..., 0:128], cq[..., 128:192], kr, z(32),
                          z(64), kr_sw, z(32), ckv], axis=-1).astype(BF16)

    nope, x1, x2 = d_w_uq[..., 0:64], d_w_uq[..., 64:80], d_w_uq[..., 80:96]
    zq = lambda n: jnp.zeros((depth, D_Q_LORA, N_HEADS, n), d_w_uq.dtype)
    main = jnp.concatenate([nope, x1, x2, zq(32)], axis=-1).reshape(depth, D_Q_LORA, 512)
    swap = jnp.concatenate([zq(64), -x2, x1, zq(32)], axis=-1).reshape(depth, D_Q_LORA, 512)
    wq = jnp.concatenate([main, swap], axis=-1)
    wq = jnp.concatenate([wq, jnp.zeros((depth, 256 - D_Q_LORA, 1024), wq.dtype)], axis=1).astype(BF16)
    gq = jnp.concatenate([d_q_norm_g, jnp.zeros((depth, 256 - D_Q_LORA), F32)], axis=-1)

    wk = jnp.concatenate([d_w_ukv[..., 0:64], jnp.zeros((depth, D_KV_LORA, N_HEADS, 64), d_w_ukv.dtype)],
                         axis=-1).reshape(depth, D_KV_LORA, 512)
    wv = d_w_ukv[..., 64:128].reshape(depth, D_KV_LORA, MIX)
    row = lambda g: g[:, None, :]
    return {
        "g_pre_mix": row(g_pre_mix), "w1": w1, "gq": row(gq), "gkv": row(d_kv_norm_g), "wq": wq,
        "wkv": jnp.concatenate([wk, wv], axis=-1).astype(BF16),
        "w_gate": w_in[..., 2400:].astype(BF16),
        "a_ln_g": row(a_ln_g), "a_ln_b": row(a_ln_b),
        "a_w_s": a_w_s.reshape(depth, N_HEADS * A_CHUNK, A_CHUNK),
        "a_b_s": jnp.repeat(jnp.swapaxes(a_b_s, 1, 2), HEAD_DIM, axis=2),
        "c_w_conv": c_w_conv, "w_branch": w_branch.astype(BF16), "w_out": w_out.astype(BF16),
        "g_post_mix": row(g_post_mix), "g_pre_mlp": row(g_pre_mlp),
        "w_mlp_in": w_mlp_in.astype(BF16), "w_mlp_out": w_mlp_out.astype(BF16),
        "g_post_mlp": row(g_post_mlp),
    }


def kernel(x, g_pre_mix, w_in, a_ln_g, a_ln_b, a_w_s, a_b_s, c_w_conv, d_q_norm_g, d_w_uq,
           d_kv_norm_g, d_w_ukv, w_branch, w_out, g_post_mix, g_pre_mlp, w_mlp_in, w_mlp_out,
           g_post_mlp):
    batch, seq, _ = x.shape
    t = batch * seq
    tab = _position_tables(seq)
    p = _pack_params(g_pre_mix, w_in, a_ln_g, a_ln_b, a_w_s, a_b_s, c_w_conv, d_q_norm_g, d_w_uq,
                     d_kv_norm_g, d_w_ukv, w_branch, w_out, g_post_mix, g_pre_mlp, w_mlp_in,
                     w_mlp_out, g_post_mlp)
    x2 = x.reshape(t, D_MODEL)
    for l in range(w_in.shape[0]):
        (auv, bqh, bql, bk, bvt, km, c3, dqt, dk, dvt) = _inproj(x2, l, p, tab, seq)
        km = km.reshape(batch, seq // MOBA_BLOCK, MIX)
        out_b = _moba(bqh, bql, bk, bvt, km, batch, seq)
        out_d = _mla(dqt, dk, dvt, batch, seq)
        x2 = _merge(x2, auv, c3, out_b, out_d, l, p, seq)
        x2 = _mlp(x2, l, p)
    return x2.reshape(batch, seq, D_MODEL)
```

```python
import functools
import math

import jax
import jax.numpy as jnp
from jax import lax
from jax.experimental import pallas as pl
from jax.experimental.pallas import tpu as pltpu

F32 = jnp.float32
BF16 = jnp.bfloat16

D_MODEL = 1024
HEAD_DIM = 64
N_HEADS = 4
MIX = 256
A_CHUNK = 128
MOBA_BLOCK = 256
MOBA_TOPK = 3
B_ROT = 16
D_NOPE, D_ROPE, D_Q_LORA, D_KV_LORA = 64, 32, 192, 128
ROPE_THETA = 500000.0
NORM_EPS = 1e-6
LN_EPS = 1e-5
LOG2E = 1.4426950408889634
MOBA_QSCALE = (HEAD_DIM ** -0.5) * LOG2E
MLA_QSCALE = ((D_NOPE + D_ROPE) ** -0.5) * LOG2E

LANE = 128
TM_IN = 512
TM_MERGE = 512
SUB_MERGE = 256
TM_MLP = 1024
SUB_MLP = 512
TQ = 256
TK = 512
L_ROWS = 16
S_SLOTS = 3
VMEM_LIMIT = 56 * 1024 * 1024

N1 = 2560
NEG_INF = float("-inf")


def _moba_cols(w):
    lead = w.shape[:-1]
    w4 = w.reshape(lead + (N_HEADS, HEAD_DIM))
    part = lambda a, b: w4[..., a:b].reshape(lead + (N_HEADS * (b - a),))
    return jnp.concatenate([part(0, 8), part(16, 40), part(8, 16), part(40, 64)], axis=-1)


def _moba_head_mask(idx, h):
    l = idx & (LANE - 1)
    return ((l >= 8 * h) & (l < 8 * h + 8)) | ((l >= 32 + 24 * h) & (l < 56 + 24 * h))


def _rms(x, g):
    return x * lax.rsqrt(jnp.mean(x * x, axis=-1, keepdims=True) + NORM_EPS) * g


def _gelu(x):
    return 0.5 * x * (1.0 + jnp.tanh(math.sqrt(2.0 / math.pi) * (x + 0.044715 * (x * x * x))))


def _layer_spec(arr, l):
    nd = arr.ndim - 1
    return pl.BlockSpec((None,) + arr.shape[1:], lambda *_: (l,) + (0,) * nd, pipeline_mode=pl.Buffered(1))


def _inproj_kernel(x_ref, tab_ref, g_ref, w1_ref, gq_ref, gkv_ref, wq_ref, wkv_ref,
                   auv_ref, bqh_ref, bql_ref, bk_ref, bvt_ref, km_ref, c3_ref,
                   dqt_ref, dk_ref, dvt_ref):
    tm = x_ref.shape[0]
    h = _rms(x_ref[...], g_ref[...]).astype(BF16)

    def proj(a, b):
        return jnp.dot(h, w1_ref[:, a:b], preferred_element_type=F32)

    auv_ref[...] = proj(0, 512)

    cb = tab_ref[:, 0:128]
    sb = tab_ref[:, 128:256]

    def rope(y):
        y0, y1 = y[:, 0:LANE], y[:, LANE:2 * LANE]
        return jnp.concatenate([y0 * cb - y1 * sb, y1 * cb + y0 * sb], axis=1)

    q = rope(proj(512, 768)) * MOBA_QSCALE
    k = rope(proj(768, 1024))
    v = proj(1024, 1280)
    qt = q.T
    qt_hi = qt.astype(BF16)
    qt_lo = (qt - qt_hi.astype(F32)).astype(BF16)
    vt = v.T.astype(BF16)
    bk_ref[...] = k.astype(BF16)
    for j in range(tm // TQ):
        bqh_ref[j] = qt_hi[:, j * TQ:(j + 1) * TQ]
        bql_ref[j] = qt_lo[:, j * TQ:(j + 1) * TQ]
    for j in range(tm // TK):
        bvt_ref[j] = vt[:, j * TK:(j + 1) * TK]
    for j in range(tm // MOBA_BLOCK):
        blk = k[j * MOBA_BLOCK:(j + 1) * MOBA_BLOCK, :]
        km_ref[0, j:j + 1, :] = jnp.sum(blk, axis=0, keepdims=True) * (1.0 / MOBA_BLOCK)

    c3_ref[...] = proj(1280, 2048)

    d = proj(2048, 2560)
    cq = d[:, 0:256]
    lane = lax.broadcasted_iota(jnp.int32, cq.shape, 1)
    ms = jnp.sum(jnp.where(lane < D_Q_LORA, cq * cq, 0.0), axis=-1, keepdims=True) * (1.0 / D_Q_LORA)
    cqn = (cq * lax.rsqrt(ms + NORM_EPS) * gq_ref[...]).astype(BF16)
    qq = jnp.dot(cqn, wq_ref[...], preferred_element_type=F32)
    cq_t = jnp.concatenate([tab_ref[:, 256:384]] * N_HEADS, axis=1)
    sq_t = jnp.concatenate([tab_ref[:, 384:512]] * N_HEADS, axis=1)
    qd = (qq[:, 0:512] * cq_t + qq[:, 512:1024] * sq_t) * MLA_QSCALE
    qdt = qd.T.astype(BF16)

    kvn = _rms(d[:, 384:512], gkv_ref[...]).astype(BF16)
    kv = jnp.dot(kvn, wkv_ref[...], preferred_element_type=F32)
    kr = d[:, 128:256] * tab_ref[:, 512:640] + d[:, 256:384] * tab_ref[:, 384:512]
    dk_ref[...] = (kv[:, 0:512] + jnp.concatenate([kr] * N_HEADS, axis=1)).astype(BF16)
    vdt = kv[:, 512:768].T.astype(BF16)
    for j in range(tm // TQ):
        dqt_ref[j] = qdt[:, j * TQ:(j + 1) * TQ]
    for j in range(tm // TK):
        dvt_ref[j] = vdt[:, j * TK:(j + 1) * TK]


def _inproj(x2, l, p, tab, seq):
    t = x2.shape[0]
    tm = TM_IN
    nt = t // tm
    spt = seq // tm
    rows = lambda w: pl.BlockSpec((tm, w), lambda i: (i, 0))
    tblk = lambda r: pl.BlockSpec((tm // TQ, r, TQ), lambda i: (i, 0, 0))
    vblk = pl.BlockSpec((tm // TK, MIX, TK), lambda i: (i, 0, 0))
    out_shape = (
        jax.ShapeDtypeStruct((t, 512), F32),
        jax.ShapeDtypeStruct((t // TQ, MIX, TQ), BF16),
        jax.ShapeDtypeStruct((t // TQ, MIX, TQ), BF16),
        jax.ShapeDtypeStruct((t, MIX), BF16),
        jax.ShapeDtypeStruct((t // TK, MIX, TK), BF16),
        jax.ShapeDtypeStruct((nt, tm // MOBA_BLOCK, MIX), F32),
        jax.ShapeDtypeStruct((t, 768), F32),
        jax.ShapeDtypeStruct((t // TQ, 512, TQ), BF16),
        jax.ShapeDtypeStruct((t, 512), BF16),
        jax.ShapeDtypeStruct((t // TK, MIX, TK), BF16),
    )
    out_specs = (
        rows(512), tblk(MIX), tblk(MIX), rows(MIX), vblk,
        pl.BlockSpec((1, tm // MOBA_BLOCK, MIX), lambda i: (i, 0, 0)),
        rows(768), tblk(512), rows(512), vblk,
    )
    params = (p["g_pre_mix"], p["w1"], p["gq"], p["gkv"], p["wq"], p["wkv"])
    in_specs = [rows(D_MODEL), pl.BlockSpec((tm, 640), lambda i: (i % spt, 0))]
    in_specs += [_layer_spec(a, l) for a in params]
    return pl.pallas_call(
        _inproj_kernel, grid=(nt,), in_specs=in_specs, out_specs=out_specs, out_shape=out_shape,
        compiler_params=pltpu.CompilerParams(dimension_semantics=("arbitrary",),
                                             vmem_limit_bytes=VMEM_LIMIT),
        name="inproj",
    )(x2, tab, *params)


def _attend(make_scores, vt_ref, s_ref, o_ref, k_windows):
    hs = [slice(h * HEAD_DIM, (h + 1) * HEAD_DIM) for h in range(N_HEADS)]
    row = lax.broadcasted_iota(jnp.int32, (MOBA_BLOCK, TQ), 0)
    col = lax.broadcasted_iota(jnp.int32, (MOBA_BLOCK, TQ), 1)
    causal = row <= col
    ones = jnp.ones((L_ROWS, TK), BF16)
    units = [(sub, h) for sub in range(2) for h in range(N_HEADS)]
    n_slots = s_ref.shape[0]

    def body(k):
        scorers = [make_scores(sub) for sub in range(2)]

        def keys_in(sub, w):
            return TK if w < k else (sub + 1) * MOBA_BLOCK

        def pass1(u):
            sub, h = units[u]
            m = None
            for w in range(k + 1):
                n = keys_in(sub, w)
                s = scorers[sub](w, h, n)
                if w == k:
                    own = jnp.where(causal, s[n - MOBA_BLOCK:n], NEG_INF)
                    s = own if n == MOBA_BLOCK else jnp.concatenate([s[0:n - MOBA_BLOCK], own], axis=0)
                s_ref[u % n_slots, w, 0:n] = s
                mw = jnp.max(s, axis=0, keepdims=True)
                m = mw if m is None else jnp.maximum(m, mw)
            return m

        def pass2(u, m):
            sub, h = units[u]
            acc = None
            for w in range(k + 1):
                n = keys_in(sub, w)
                p = jnp.exp2(s_ref[u % n_slots, w, 0:n] - m).astype(BF16)
                v_aug = jnp.concatenate([vt_ref[w, hs[h], 0:n], ones[:, 0:n]], axis=0)
                r = jnp.dot(v_aug, p, preferred_element_type=F32)
                acc = r if acc is None else acc + r
            return acc[0:HEAD_DIM, :] * (1.0 / acc[HEAD_DIM:HEAD_DIM + 1, :])

        m = [pass1(0)]
        out = []
        for u in range(1, len(units)):
            m.append(pass1(u))
            out.append(pass2(u - 1, m[u - 1]))
        out.append(pass2(len(units) - 1, m[-1]))
        for sub in range(2):
            tile = jnp.concatenate(out[sub * N_HEADS:(sub + 1) * N_HEADS], axis=0).T
            o_ref[sub * TQ:(sub + 1) * TQ, :] = tile.astype(o_ref.dtype)

    for k in range(s_ref.shape[1]):
        pl.when(k_windows == k)(functools.partial(body, k))


def _moba_kernel(qh_ref, ql_ref, k_ref, vt_ref, km_ref, o_ref, qm_s, s_ref):
    i2 = pl.program_id(1)
    row = lax.broadcasted_iota(jnp.int32, (MIX, TQ), 0)
    for sub in range(2):
        qt_hi = qh_ref[sub]
        for h in range(N_HEADS):
            qm_s[sub, h] = jnp.where(_moba_head_mask(row, h), qt_hi, jnp.zeros_like(qt_hi))

    def make_scores(sub):
        i = 2 * i2 + sub
        km = km_ref[0]
        nb = km.shape[0]
        lane = lax.broadcasted_iota(jnp.int32, km.shape, 1)
        kmh = jnp.concatenate([jnp.where(_moba_head_mask(lane, h), km, 0.0) for h in range(N_HEADS)], axis=0)
        km_hi = kmh.astype(BF16)
        km_lo = (kmh - km_hi.astype(F32)).astype(BF16)
        gate = (jnp.dot(km_hi, qh_ref[sub], preferred_element_type=F32)
                + jnp.dot(km_hi, ql_ref[sub], preferred_element_type=F32)
                + jnp.dot(km_lo, qh_ref[sub], preferred_element_type=F32))

        blk = lax.broadcasted_iota(jnp.int32, (nb, TQ), 0)
        blkf = blk.astype(F32)
        biases = []
        for h in range(N_HEADS):
            g = jnp.where(blk < i, gate[h * nb:(h + 1) * nb, :], NEG_INF)
            bias = jnp.full((nb, TQ), NEG_INF, F32)
            for _ in range(MOBA_TOPK):
                mx = jnp.max(g, axis=0, keepdims=True)
                first = jnp.min(jnp.where(g == mx, blkf, float(nb)), axis=0, keepdims=True)
                hit = blkf == first
                bias = jnp.where(hit & (mx > NEG_INF), 0.0, bias)
                g = jnp.where(hit, NEG_INF, g)
            biases.append(jnp.where(blk == i, 0.0, bias))

        def scores(w, h, n):
            s = jnp.dot(k_ref[w * TK:w * TK + n, :], qm_s[sub, h], preferred_element_type=F32)
            b = biases[h]
            parts = [s[j * MOBA_BLOCK:(j + 1) * MOBA_BLOCK] + b[2 * w + j:2 * w + j + 1, :]
                     for j in range(n // MOBA_BLOCK)]
            return parts[0] if len(parts) == 1 else jnp.concatenate(parts, axis=0)

        return scores

    _attend(make_scores, vt_ref, s_ref, o_ref, i2)


def _moba(qh, ql, k, vt, km, batch, seq):
    nq2 = seq // (2 * TQ)
    t = batch * seq
    qspec = pl.BlockSpec((2, MIX, TQ), lambda b, i: (b * nq2 + i, 0, 0))
    return pl.pallas_call(
        _moba_kernel, grid=(batch, nq2),
        in_specs=[qspec, qspec,
                  pl.BlockSpec((seq, MIX), lambda b, i: (b, 0)),
                  pl.BlockSpec((seq // TK, MIX, TK), lambda b, i: (b, 0, 0)),
                  pl.BlockSpec((1, seq // MOBA_BLOCK, MIX), lambda b, i: (b, 0, 0))],
        out_specs=pl.BlockSpec((2 * TQ, MIX), lambda b, i: (b * nq2 + i, 0)),
        out_shape=jax.ShapeDtypeStruct((t, MIX), BF16),
        scratch_shapes=[pltpu.VMEM((2, N_HEADS, MIX, TQ), BF16),
                        pltpu.VMEM((S_SLOTS, seq // TK, TK, TQ), F32)],
        compiler_params=pltpu.CompilerParams(dimension_semantics=("arbitrary", "arbitrary"),
                                             vmem_limit_bytes=VMEM_LIMIT),
        name="moba",
    )(qh, ql, k, vt, km)


def _mla_kernel(qt_ref, k_ref, vt_ref, o_ref, s_ref):
    def make_scores(sub):
        def scores(w, h, n):
            kw = k_ref[w * TK:w * TK + n, h * LANE:(h + 1) * LANE]
            return jnp.dot(kw, qt_ref[sub, h * LANE:(h + 1) * LANE, :], preferred_element_type=F32)
        return scores

    _attend(make_scores, vt_ref, s_ref, o_ref, pl.program_id(1))


def _mla(qt, k, vt, batch, seq):
    nq2 = seq // (2 * TQ)
    t = batch * seq
    return pl.pallas_call(
        _mla_kernel, grid=(batch, nq2),
        in_specs=[pl.BlockSpec((2, 512, TQ), lambda b, i: (b * nq2 + i, 0, 0)),
                  pl.BlockSpec((seq, 512), lambda b, i: (b, 0)),
                  pl.BlockSpec((seq // TK, MIX, TK), lambda b, i: (b, 0, 0))],
        out_specs=pl.BlockSpec((2 * TQ, MIX), lambda b, i: (b * nq2 + i, 0)),
        out_shape=jax.ShapeDtypeStruct((t, MIX), BF16),
        scratch_shapes=[pltpu.VMEM((S_SLOTS, seq // TK, TK, TQ), F32)],
        compiler_params=pltpu.CompilerParams(dimension_semantics=("arbitrary", "arbitrary"),
                                             vmem_limit_bytes=VMEM_LIMIT),
        name="mla",
    )(qt, k, vt)


def _merge_rows(r0, first_in_seq, x_ref, auv_ref, c3_ref, c3p_ref, ob_ref, od_ref,
                g_ref, wg_ref, lng_ref, lnb_ref, ws, bs_ref, wc_ref, wb_ref, wo_ref, gp_ref, o_ref):
    rows = slice(r0, r0 + SUB_MERGE)
    y_b = jnp.dot(ob_ref[rows, :], wb_ref[1], preferred_element_type=F32)
    y_d = jnp.dot(od_ref[rows, :], wb_ref[3], preferred_element_type=F32)
    x = x_ref[rows, :]
    h = _rms(x, g_ref[...]).astype(BF16)

    u = _gelu(auv_ref[rows, 0:MIX])
    v = _gelu(auv_ref[rows, MIX:2 * MIX])
    mu = jnp.mean(v, axis=-1, keepdims=True)
    vc = v - mu
    var = jnp.mean(vc * vc, axis=-1, keepdims=True)
    v = (vc * lax.rsqrt(var + LN_EPS) * lng_ref[...] + lnb_ref[...]).astype(BF16)
    lane = lax.broadcasted_iota(jnp.int32, (A_CHUNK, MIX), 1)
    svs = []
    for cidx in range(SUB_MERGE // A_CHUNK):
        r = jnp.dot(ws, v[cidx * A_CHUNK:(cidx + 1) * A_CHUNK, :], preferred_element_type=F32)
        sv = jnp.where(lane < 64, r[0:128],
                       jnp.where(lane < 128, r[128:256], jnp.where(lane < 192, r[256:384], r[384:512])))
        svs.append(sv + bs_ref[...])
    out_a = (u * jnp.concatenate(svs, axis=0)).astype(BF16)

    z = c3_ref[rows, MIX:2 * MIX] * c3_ref[rows, 2 * MIX:3 * MIX]
    if r0 == 0:
        zp = c3p_ref[:, MIX:2 * MIX] * c3p_ref[:, 2 * MIX:3 * MIX]
        zp = jnp.where(first_in_seq, 0.0, zp)
    else:
        zp = c3_ref[r0 - 8:r0, MIX:2 * MIX] * c3_ref[r0 - 8:r0, 2 * MIX:3 * MIX]
    zm1, zm2 = zp[7:8, :], zp[6:7, :]
    rowi = lax.broadcasted_iota(jnp.int32, z.shape, 0)
    z1 = jnp.where(rowi == 0, zm1, pltpu.roll(z, 1, axis=0))
    z2 = jnp.where(rowi == 0, zm2, jnp.where(rowi == 1, zm1, pltpu.roll(z, 2, axis=0)))
    conv = wc_ref[0:1, :] * z2 + wc_ref[1:2, :] * z1 + wc_ref[2:3, :] * z
    out_c = (c3_ref[rows, 0:MIX] * conv).astype(BF16)

    def gate(bi):
        logits = jnp.dot(h, wg_ref[:, bi * D_MODEL:(bi + 1) * D_MODEL], preferred_element_type=F32)
        return 1.0 / (1.0 + jnp.exp(-logits))

    merged = gate(1) * y_b + gate(3) * y_d
    merged = merged + gate(0) * jnp.dot(out_a, wb_ref[0], preferred_element_type=F32)
    merged = merged + gate(2) * jnp.dot(out_c, wb_ref[2], preferred_element_type=F32)
    y = jnp.dot(merged.astype(BF16), wo_ref[...], preferred_element_type=F32)
    o_ref[rows, :] = x + _rms(y, gp_ref[...])


def _merge_kernel(x_ref, auv_ref, c3_ref, c3p_ref, ob_ref, od_ref,
                  g_ref, wg_ref, lng_ref, lnb_ref, ws_ref, bs_ref, wc_ref, wb_ref, wo_ref, gp_ref,
                  o_ref, *, steps_per_seq):
    wrow = lax.broadcasted_iota(jnp.int32, ws_ref.shape, 0) & (A_CHUNK - 1)
    wcol = lax.broadcasted_iota(jnp.int32, ws_ref.shape, 1)
    ws = jnp.where(wrow >= wcol, ws_ref[...], 0.0).astype(BF16)
    first_in_seq = pl.program_id(0) % steps_per_seq == 0
    for r0 in range(0, x_ref.shape[0], SUB_MERGE):
        _merge_rows(r0, first_in_seq, x_ref, auv_ref, c3_ref, c3p_ref, ob_ref, od_ref,
                    g_ref, wg_ref, lng_ref, lnb_ref, ws, bs_ref, wc_ref, wb_ref, wo_ref, gp_ref, o_ref)


def _merge(x2, auv, c3, ob, od, l, p, seq):
    t = x2.shape[0]
    tm = TM_MERGE
    rows = lambda w: pl.BlockSpec((tm, w), lambda i: (i, 0))
    prev8 = pl.BlockSpec((8, 768), lambda i: (jnp.maximum(i * (tm // 8) - 1, 0), 0))
    params = (p["g_pre_mix"], p["w_gate"], p["a_ln_g"], p["a_ln_b"], p["a_w_s"], p["a_b_s"],
              p["c_w_conv"], p["w_branch"], p["w_out"], p["g_post_mix"])
    in_specs = [rows(D_MODEL), rows(512), rows(768), prev8, rows(MIX), rows(MIX)]
    in_specs += [_layer_spec(a, l) for a in params]
    return pl.pallas_call(
        functools.partial(_merge_kernel, steps_per_seq=seq // tm), grid=(t // tm,),
        in_specs=in_specs, out_specs=rows(D_MODEL),
        out_shape=jax.ShapeDtypeStruct((t, D_MODEL), F32),
        compiler_params=pltpu.CompilerParams(dimension_semantics=("arbitrary",),
                                             vmem_limit_bytes=VMEM_LIMIT),
        name="merge",
    )(x2, auv, c3, c3, ob, od, *params)


def _mlp_kernel(x_ref, g_ref, w1_ref, w2_ref, gp_ref, o_ref):
    d_ff = w1_ref.shape[1]
    for r0 in range(0, x_ref.shape[0], SUB_MLP):
        rows = slice(r0, r0 + SUB_MLP)
        x = x_ref[rows, :]
        h = _rms(x, g_ref[...]).astype(BF16)
        y = None
        for cidx in range(d_ff // D_MODEL):
            sl = slice(cidx * D_MODEL, (cidx + 1) * D_MODEL)
            a = jnp.maximum(jnp.dot(h, w1_ref[:, sl], preferred_element_type=F32), 0.0)
            part = jnp.dot((a * a).astype(BF16), w2_ref[sl, :], preferred_element_type=F32)
            y = part if y is None else y + part
        o_ref[rows, :] = x + _rms(y, gp_ref[...])


def _mlp(x2, l, p):
    t = x2.shape[0]
    tm = TM_MLP
    rows = pl.BlockSpec((tm, D_MODEL), lambda i: (i, 0))
    params = (p["g_pre_mlp"], p["w_mlp_in"], p["w_mlp_out"], p["g_post_mlp"])
    return pl.pallas_call(
        _mlp_kernel, grid=(t // tm,),
        in_specs=[rows] + [_layer_spec(a, l) for a in params],
        out_specs=rows, out_shape=jax.ShapeDtypeStruct((t, D_MODEL), F32),
        compiler_params=pltpu.CompilerParams(dimension_semantics=("arbitrary",),
                                             vmem_limit_bytes=VMEM_LIMIT),
        name="mlp",
    )(x2, *params)


def _rope_tables(dim, seq):
    inv = 1.0 / (ROPE_THETA ** (jnp.arange(0, dim, 2, dtype=F32) / dim))
    ang = jnp.arange(seq, dtype=F32)[:, None] * inv[None, :]
    return jnp.cos(ang), jnp.sin(ang)


def _position_tables(seq):
    cos_b, sin_b = _rope_tables(B_ROT, seq)
    cos_d, sin_d = _rope_tables(D_ROPE, seq)
    one, zero = jnp.ones((seq, 1), F32), jnp.zeros((seq, 1), F32)
    cb = jnp.concatenate([jnp.tile(cos_b, (1, N_HEADS)), jnp.tile(one, (1, 96))], axis=1)
    sb = jnp.concatenate([jnp.tile(sin_b, (1, N_HEADS)), jnp.tile(zero, (1, 96))], axis=1)
    cq = jnp.concatenate([jnp.tile(one, (1, 64)), cos_d, cos_d, jnp.tile(zero, (1, 32))], axis=1)
    sq = jnp.concatenate([jnp.tile(zero, (1, 64)), sin_d, sin_d, jnp.tile(zero, (1, 32))], axis=1)
    ck = jnp.concatenate([jnp.tile(zero, (1, 64)), cos_d, cos_d, jnp.tile(zero, (1, 32))], axis=1)
    return jnp.concatenate([cb, sb, cq, sq, ck], axis=1)


def _pack_params(g_pre_mix, w_in, a_ln_g, a_ln_b, a_w_s, a_b_s, c_w_conv, d_q_norm_g, d_w_uq, d_kv_norm_g,
                 d_w_ukv, w_branch, w_out, g_post_mix, g_pre_mlp, w_mlp_in, w_mlp_out, g_post_mlp):
    depth = w_in.shape[0]
    z = lambda n: jnp.zeros((depth, D_MODEL, n), w_in.dtype)
    a = w_in[..., 0:512]
    bq, bk, bv = w_in[..., 512:768], w_in[..., 768:1024], w_in[..., 1024:1280]
    c = w_in[..., 1280:2048]
    cq, ckv, kr = w_in[..., 2048:2240], w_in[..., 2240:2368], w_in[..., 2368:2400]
    kr_sw = jnp.concatenate([-kr[..., 16:32], kr[..., 0:16]], axis=-1)
    w1 = jnp.concatenate([a, _moba_cols(bq), _moba_cols(bk), bv, c,
                          cq[..., 0:128], cq[..., 128:192], kr, z(32),
                          z(64), kr_sw, z(32), ckv], axis=-1).astype(BF16)

    nope, x1, x2 = d_w_uq[..., 0:64], d_w_uq[..., 64:80], d_w_uq[..., 80:96]
    zq = lambda n: jnp.zeros((depth, D_Q_LORA, N_HEADS, n), d_w_uq.dtype)
    main = jnp.concatenate([nope, x1, x2, zq(32)], axis=-1).reshape(depth, D_Q_LORA, 512)
    swap = jnp.concatenate([zq(64), -x2, x1, zq(32)], axis=-1).reshape(depth, D_Q_LORA, 512)
    wq = jnp.concatenate([main, swap], axis=-1)
    wq = jnp.concatenate([wq, jnp.zeros((depth, 256 - D_Q_LORA, 1024), wq.dtype)], axis=1).astype(BF16)
    gq = jnp.concatenate([d_q_norm_g, jnp.zeros((depth, 256 - D_Q_LORA), F32)], axis=-1)

    wk = jnp.concatenate([d_w_ukv[..., 0:64], jnp.zeros((depth, D_KV_LORA, N_HEADS, 64), d_w_ukv.dtype)],
                         axis=-1).reshape(depth, D_KV_LORA, 512)
    wv = d_w_ukv[..., 64:128].reshape(depth, D_KV_LORA, MIX)
    row = lambda g: g[:, None, :]
    return {
        "g_pre_mix": row(g_pre_mix), "w1": w1, "gq": row(gq), "gkv": row(d_kv_norm_g), "wq": wq,
        "wkv": jnp.concatenate([wk, wv], axis=-1).astype(BF16),
        "w_gate": w_in[..., 2400:].astype(BF16),
        "a_ln_g": row(a_ln_g), "a_ln_b": row(a_ln_b),
        "a_w_s": a_w_s.reshape(depth, N_HEADS * A_CHUNK, A_CHUNK),
        "a_b_s": jnp.repeat(jnp.swapaxes(a_b_s, 1, 2), HEAD_DIM, axis=2),
        "c_w_conv": c_w_conv, "w_branch": w_branch.astype(BF16), "w_out": w_out.astype(BF16),
        "g_post_mix": row(g_post_mix), "g_pre_mlp": row(g_pre_mlp),
        "w_mlp_in": w_mlp_in.astype(BF16), "w_mlp_out": w_mlp_out.astype(BF16),
        "g_post_mlp": row(g_post_mlp),
    }


def kernel(x, g_pre_mix, w_in, a_ln_g, a_ln_b, a_w_s, a_b_s, c_w_conv, d_q_norm_g, d_w_uq,
           d_kv_norm_g, d_w_ukv, w_branch, w_out, g_post_mix, g_pre_mlp, w_mlp_in, w_mlp_out,
           g_post_mlp):
    batch, seq, _ = x.shape
    t = batch * seq
    tab = _position_tables(seq)
    p = _pack_params(g_pre_mix, w_in, a_ln_g, a_ln_b, a_w_s, a_b_s, c_w_conv, d_q_norm_g, d_w_uq,
                     d_kv_norm_g, d_w_ukv, w_branch, w_out, g_post_mix, g_pre_mlp, w_mlp_in,
                     w_mlp_out, g_post_mlp)
    x2 = x.reshape(t, D_MODEL)
    for l in range(w_in.shape[0]):
        (auv, bqh, bql, bk, bvt, km, c3, dqt, dk, dvt) = _inproj(x2, l, p, tab, seq)
        km = km.reshape(batch, seq // MOBA_BLOCK, MIX)
        out_b = _moba(bqh, bql, bk, bvt, km, batch, seq)
        out_d = _mla(dqt, dk, dvt, batch, seq)
        x2 = _merge(x2, auv, c3, out_b, out_d, l, p, seq)
        x2 = _mlp(x2, l, p)
    return x2.reshape(batch, seq, D_MODEL)
```

```python
import functools
import math

import jax
import jax.numpy as jnp
from jax import lax
from jax.experimental import pallas as pl
from jax.experimental.pallas import tpu as pltpu

F32 = jnp.float32
BF16 = jnp.bfloat16

D_MODEL = 1024
HEAD_DIM = 64
N_HEADS = 4
MIX = 256
A_CHUNK = 128
MOBA_BLOCK = 256
MOBA_TOPK = 3
B_ROT = 16
D_NOPE, D_ROPE, D_Q_LORA, D_KV_LORA = 64, 32, 192, 128
ROPE_THETA = 500000.0
NORM_EPS = 1e-6
LN_EPS = 1e-5
LOG2E = 1.4426950408889634
MOBA_QSCALE = (HEAD_DIM ** -0.5) * LOG2E
MLA_QSCALE = ((D_NOPE + D_ROPE) ** -0.5) * LOG2E

LANE = 128
TM_IN = 512
SUB_IN = 512
TM_MERGE = 512
SUB_MERGE = 256
TM_MLP = 1024
SUB_MLP = 512
TQ = 256
TK = 512
L_ROWS = 16
PIPE_DEPTH = 4
S_SLOTS = PIPE_DEPTH + 2
VMEM_LIMIT = 56 * 1024 * 1024

N1 = 2560
NEG_INF = float("-inf")


def _moba_cols(w):
    lead = w.shape[:-1]
    w4 = w.reshape(lead + (N_HEADS, HEAD_DIM))
    part = lambda a, b: w4[..., a:b].reshape(lead + (N_HEADS * (b - a),))
    return jnp.concatenate([part(0, 8), part(16, 40), part(8, 16), part(40, 64)], axis=-1)


def _moba_head_mask(idx, h):
    l = idx & (LANE - 1)
    return ((l >= 8 * h) & (l < 8 * h + 8)) | ((l >= 32 + 24 * h) & (l < 56 + 24 * h))


def _rms(x, g):
    return x * lax.rsqrt(jnp.mean(x * x, axis=-1, keepdims=True) + NORM_EPS) * g


def _gelu(x):
    return 0.5 * x * (1.0 + jnp.tanh(math.sqrt(2.0 / math.pi) * (x + 0.044715 * (x * x * x))))


def _layer_spec(arr, l):
    nd = arr.ndim - 1
    return pl.BlockSpec((None,) + arr.shape[1:], lambda *_: (l,) + (0,) * nd, pipeline_mode=pl.Buffered(1))


def _inproj_rows(blk, x_ref, tab_ref, g_ref, w1_ref, gq_ref, gkv_ref, wq_ref, wkv_ref,
                 auv_ref, bqh_ref, bql_ref, bk_ref, bvt_ref, km_ref, c3_ref, dqt_ref, dk_ref, dvt_ref):
    tm = SUB_IN
    rows = slice(blk * tm, (blk + 1) * tm)
    h = _rms(x_ref[rows, :], g_ref[...]).astype(BF16)

    def proj(a, b):
        return jnp.dot(h, w1_ref[:, a:b], preferred_element_type=F32)

    auv_ref[rows, :] = proj(0, 512)

    cb = tab_ref[rows, 0:128]
    sb = tab_ref[rows, 128:256]

    def rope(y):
        y0, y1 = y[:, 0:LANE], y[:, LANE:2 * LANE]
        return jnp.concatenate([y0 * cb - y1 * sb, y1 * cb + y0 * sb], axis=1)

    q = rope(proj(512, 768)) * MOBA_QSCALE
    k = rope(proj(768, 1024))
    v = proj(1024, 1280)
    qt = q.T
    qt_hi = qt.astype(BF16)
    qt_lo = (qt - qt_hi.astype(F32)).astype(BF16)
    vt = v.T.astype(BF16)
    bk_ref[rows, :] = k.astype(BF16)
    for j in range(tm // TQ):
        bqh_ref[blk * (tm // TQ) + j] = qt_hi[:, j * TQ:(j + 1) * TQ]
        bql_ref[blk * (tm // TQ) + j] = qt_lo[:, j * TQ:(j + 1) * TQ]
    for j in range(tm // TK):
        bvt_ref[blk * (tm // TK) + j] = vt[:, j * TK:(j + 1) * TK]
    for j in range(tm // MOBA_BLOCK):
        kb = k[j * MOBA_BLOCK:(j + 1) * MOBA_BLOCK, :]
        jj = blk * (tm // MOBA_BLOCK) + j
        km_ref[0, jj:jj + 1, :] = jnp.sum(kb, axis=0, keepdims=True) * (1.0 / MOBA_BLOCK)

    c3_ref[rows, :] = proj(1280, 2048)

    d = proj(2048, 2560)
    cq = d[:, 0:256]
    lane = lax.broadcasted_iota(jnp.int32, cq.shape, 1)
    ms = jnp.sum(jnp.where(lane < D_Q_LORA, cq * cq, 0.0), axis=-1, keepdims=True) * (1.0 / D_Q_LORA)
    cqn = (cq * lax.rsqrt(ms + NORM_EPS) * gq_ref[...]).astype(BF16)
    qq = jnp.dot(cqn, wq_ref[...], preferred_element_type=F32)
    cq_t = jnp.concatenate([tab_ref[rows, 256:384]] * N_HEADS, axis=1)
    sq_t = jnp.concatenate([tab_ref[rows, 384:512]] * N_HEADS, axis=1)
    qd = (qq[:, 0:512] * cq_t + qq[:, 512:1024] * sq_t) * MLA_QSCALE
    qdt = qd.T.astype(BF16)

    kvn = _rms(d[:, 384:512], gkv_ref[...]).astype(BF16)
    kv = jnp.dot(kvn, wkv_ref[...], preferred_element_type=F32)
    kr = d[:, 128:256] * tab_ref[rows, 512:640] + d[:, 256:384] * tab_ref[rows, 384:512]
    dk_ref[rows, :] = (kv[:, 0:512] + jnp.concatenate([kr] * N_HEADS, axis=1)).astype(BF16)
    vdt = kv[:, 512:768].T.astype(BF16)
    for j in range(tm // TQ):
        dqt_ref[blk * (tm // TQ) + j] = qdt[:, j * TQ:(j + 1) * TQ]
    for j in range(tm // TK):
        dvt_ref[blk * (tm // TK) + j] = vdt[:, j * TK:(j + 1) * TK]


def _inproj_kernel(x_ref, *refs):
    for blk in range(x_ref.shape[0] // SUB_IN):
        _inproj_rows(blk, x_ref, *refs)


def _inproj(x2, l, p, tab, seq):
    t = x2.shape[0]
    tm = TM_IN
    nt = t // tm
    spt = seq // tm
    rows = lambda w: pl.BlockSpec((tm, w), lambda i: (i, 0))
    tblk = lambda r: pl.BlockSpec((tm // TQ, r, TQ), lambda i: (i, 0, 0))
    vblk = pl.BlockSpec((tm // TK, MIX, TK), lambda i: (i, 0, 0))
    out_shape = (
        jax.ShapeDtypeStruct((t, 512), F32),
        jax.ShapeDtypeStruct((t // TQ, MIX, TQ), BF16),
        jax.ShapeDtypeStruct((t // TQ, MIX, TQ), BF16),
        jax.ShapeDtypeStruct((t, MIX), BF16),
        jax.ShapeDtypeStruct((t // TK, MIX, TK), BF16),
        jax.ShapeDtypeStruct((nt, tm // MOBA_BLOCK, MIX), F32),
        jax.ShapeDtypeStruct((t, 768), F32),
        jax.ShapeDtypeStruct((t // TQ, 512, TQ), BF16),
        jax.ShapeDtypeStruct((t, 512), BF16),
        jax.ShapeDtypeStruct((t // TK, MIX, TK), BF16),
    )
    out_specs = (
        rows(512), tblk(MIX), tblk(MIX), rows(MIX), vblk,
        pl.BlockSpec((1, tm // MOBA_BLOCK, MIX), lambda i: (i, 0, 0)),
        rows(768), tblk(512), rows(512), vblk,
    )
    params = (p["g_pre_mix"], p["w1"], p["gq"], p["gkv"], p["wq"], p["wkv"])
    in_specs = [rows(D_MODEL), pl.BlockSpec((tm, 640), lambda i: (i % spt, 0))]
    in_specs += [_layer_spec(a, l) for a in params]
    return pl.pallas_call(
        _inproj_kernel, grid=(nt,), in_specs=in_specs, out_specs=out_specs, out_shape=out_shape,
        compiler_params=pltpu.CompilerParams(dimension_semantics=("arbitrary",),
                                             vmem_limit_bytes=VMEM_LIMIT),
        name="inproj",
    )(x2, tab, *params)


def _attend(make_scores, vt_ref, s_ref, o_ref, k_windows):
    hs = [slice(h * HEAD_DIM, (h + 1) * HEAD_DIM) for h in range(N_HEADS)]
    row = lax.broadcasted_iota(jnp.int32, (MOBA_BLOCK, TQ), 0)
    col = lax.broadcasted_iota(jnp.int32, (MOBA_BLOCK, TQ), 1)
    causal = row <= col
    ones = jnp.ones((L_ROWS, TK), BF16)
    units = [(sub, h) for sub in range(2) for h in range(N_HEADS)]
    n_slots = s_ref.shape[0]

    def body(k):
        scorers = [make_scores(sub) for sub in range(2)]
        m, acc, out = {}, {}, {}

        def keys_in(sub, w):
            return TK if w < k else (sub + 1) * MOBA_BLOCK

        def score_step(t, u, w):
            sub, h = units[u]
            n = keys_in(sub, w)
            s, bias = scorers[sub](w, h, n)
            if w == k:
                own = jnp.where(causal, s[n - MOBA_BLOCK:n], NEG_INF)
                s = own if n == MOBA_BLOCK else jnp.concatenate([s[0:n - MOBA_BLOCK], own], axis=0)
            s_ref[t % n_slots, 0:n] = s
            mw = None
            for j in range(n // MOBA_BLOCK):
                mb = jnp.max(s[j * MOBA_BLOCK:(j + 1) * MOBA_BLOCK], axis=0, keepdims=True)
                mb = mb if bias is None else mb + bias[j]
                mw = mb if mw is None else jnp.maximum(mw, mb)
            return mw, bias

        def value_step(t, u, w, mw_bias):
            sub, h = units[u]
            n = keys_in(sub, w)
            mw, bias = mw_bias
            m_new = mw if u not in m else jnp.maximum(m[u], mw)
            if bias is None:
                p = jnp.exp2(s_ref[t % n_slots, 0:n] - m_new)
            else:
                p = jnp.concatenate(
                    [jnp.exp2(s_ref[t % n_slots, j * MOBA_BLOCK:(j + 1) * MOBA_BLOCK] - (m_new - bias[j]))
                     for j in range(n // MOBA_BLOCK)], axis=0)
            p = p.astype(BF16)
            v_aug = jnp.concatenate([vt_ref[w, hs[h], 0:n], ones[:, 0:n]], axis=0)
            r = jnp.dot(v_aug, p, preferred_element_type=F32)
            acc[u] = r if u not in acc else jnp.exp2(m[u] - m_new) * acc[u] + r
            m[u] = m_new
            if w == last_window:
                out[u] = acc[u][0:HEAD_DIM, :] * (1.0 / acc[u][HEAD_DIM:HEAD_DIM + 1, :])

        windows = [k] + list(range(k))
        last_window = windows[-1]
        steps = [(u, w) for u in range(len(units)) for w in windows]
        queue = []
        for t, (u, w) in enumerate(steps):
            queue.append((t, u, w, score_step(t, u, w)))
            if len(queue) > PIPE_DEPTH:
                value_step(*queue.pop(0))
        while queue:
            value_step(*queue.pop(0))
        for sub in range(2):
            tile = jnp.concatenate([out[sub * N_HEADS + h] for h in range(N_HEADS)], axis=0).T
            o_ref[sub * TQ:(sub + 1) * TQ, :] = tile.astype(o_ref.dtype)

    for k in range(vt_ref.shape[0]):
        pl.when(k_windows == k)(functools.partial(body, k))


def _moba_kernel(qh_ref, ql_ref, k_ref, vt_ref, km_ref, o_ref, qm_s, s_ref):
    i2 = pl.program_id(1)
    row = lax.broadcasted_iota(jnp.int32, (MIX, TQ), 0)
    for sub in range(2):
        qt_hi = qh_ref[sub]
        for h in range(N_HEADS):
            qm_s[sub, h] = jnp.where(_moba_head_mask(row, h), qt_hi, jnp.zeros_like(qt_hi))

    def make_scores(sub):
        i = 2 * i2 + sub
        km = km_ref[0]
        nb = km.shape[0]
        lane = lax.broadcasted_iota(jnp.int32, km.shape, 1)
        kmh = jnp.concatenate([jnp.where(_moba_head_mask(lane, h), km, 0.0) for h in range(N_HEADS)], axis=0)
        km_hi = kmh.astype(BF16)
        km_lo = (kmh - km_hi.astype(F32)).astype(BF16)
        gate = (jnp.dot(km_hi, qh_ref[sub], preferred_element_type=F32)
                + jnp.dot(km_hi, ql_ref[sub], preferred_element_type=F32)
                + jnp.dot(km_lo, qh_ref[sub], preferred_element_type=F32))

        blk = lax.broadcasted_iota(jnp.int32, (nb, TQ), 0)
        blkf = blk.astype(F32)
        biases = []
        for h in range(N_HEADS):
            g = jnp.where(blk < i, gate[h * nb:(h + 1) * nb, :], NEG_INF)
            bias = jnp.full((nb, TQ), NEG_INF, F32)
            for _ in range(MOBA_TOPK):
                mx = jnp.max(g, axis=0, keepdims=True)
                first = jnp.min(jnp.where(g == mx, blkf, float(nb)), axis=0, keepdims=True)
                hit = blkf == first
                bias = jnp.where(hit & (mx > NEG_INF), 0.0, bias)
                g = jnp.where(hit, NEG_INF, g)
            biases.append(jnp.where(blk == i, 0.0, bias))

        def scores(w, h, n):
            s = jnp.dot(k_ref[w * TK:w * TK + n, :], qm_s[sub, h], preferred_element_type=F32)
            b = biases[h]
            return s, [b[2 * w + j:2 * w + j + 1, :] for j in range(n // MOBA_BLOCK)]

        return scores

    _attend(make_scores, vt_ref, s_ref, o_ref, i2)


def _moba(qh, ql, k, vt, km, batch, seq):
    nq2 = seq // (2 * TQ)
    t = batch * seq
    qspec = pl.BlockSpec((2, MIX, TQ), lambda b, i: (b * nq2 + i, 0, 0))
    return pl.pallas_call(
        _moba_kernel, grid=(batch, nq2),
        in_specs=[qspec, qspec,
                  pl.BlockSpec((seq, MIX), lambda b, i: (b, 0)),
                  pl.BlockSpec((seq // TK, MIX, TK), lambda b, i: (b, 0, 0)),
                  pl.BlockSpec((1, seq // MOBA_BLOCK, MIX), lambda b, i: (b, 0, 0))],
        out_specs=pl.BlockSpec((2 * TQ, MIX), lambda b, i: (b * nq2 + i, 0)),
        out_shape=jax.ShapeDtypeStruct((t, MIX), BF16),
        scratch_shapes=[pltpu.VMEM((2, N_HEADS, MIX, TQ), BF16),
                        pltpu.VMEM((S_SLOTS, TK, TQ), F32)],
        compiler_params=pltpu.CompilerParams(dimension_semantics=("arbitrary", "arbitrary"),
                                             vmem_limit_bytes=VMEM_LIMIT),
        name="moba",
    )(qh, ql, k, vt, km)


def _mla_kernel(qt_ref, k_ref, vt_ref, o_ref, s_ref):
    def make_scores(sub):
        def scores(w, h, n):
            kw = k_ref[w * TK:w * TK + n, h * LANE:(h + 1) * LANE]
            return jnp.dot(kw, qt_ref[sub, h * LANE:(h + 1) * LANE, :], preferred_element_type=F32), None
        return scores

    _attend(make_scores, vt_ref, s_ref, o_ref, pl.program_id(1))


def _mla(qt, k, vt, batch, seq):
    nq2 = seq // (2 * TQ)
    t = batch * seq
    return pl.pallas_call(
        _mla_kernel, grid=(batch, nq2),
        in_specs=[pl.BlockSpec((2, 512, TQ), lambda b, i: (b * nq2 + i, 0, 0)),
                  pl.BlockSpec((seq, 512), lambda b, i: (b, 0)),
                  pl.BlockSpec((seq // TK, MIX, TK), lambda b, i: (b, 0, 0))],
        out_specs=pl.BlockSpec((2 * TQ, MIX), lambda b, i: (b * nq2 + i, 0)),
        out_shape=jax.ShapeDtypeStruct((t, MIX), BF16),
        scratch_shapes=[pltpu.VMEM((S_SLOTS, TK, TQ), F32)],
        compiler_params=pltpu.CompilerParams(dimension_semantics=("arbitrary", "arbitrary"),
                                             vmem_limit_bytes=VMEM_LIMIT),
        name="mla",
    )(qt, k, vt)


def _merge_rows(r0, first_in_seq, x_ref, auv_ref, c3_ref, c3p_ref, ob_ref, od_ref,
                g_ref, wg_ref, lng_ref, lnb_ref, ws, bs_ref, wc_ref, wb_ref, wo_ref, gp_ref, o_ref):
    rows = slice(r0, r0 + SUB_MERGE)
    y_b = jnp.dot(ob_ref[rows, :], wb_ref[1], preferred_element_type=F32)
    y_d = jnp.dot(od_ref[rows, :], wb_ref[3], preferred_element_type=F32)
    x = x_ref[rows, :]
    h = _rms(x, g_ref[...]).astype(BF16)

    u = _gelu(auv_ref[rows, 0:MIX])
    v = _gelu(auv_ref[rows, MIX:2 * MIX])
    mu = jnp.mean(v, axis=-1, keepdims=True)
    vc = v - mu
    var = jnp.mean(vc * vc, axis=-1, keepdims=True)
    v = (vc * lax.rsqrt(var + LN_EPS) * lng_ref[...] + lnb_ref[...]).astype(BF16)
    lane = lax.broadcasted_iota(jnp.int32, (A_CHUNK, MIX), 1)
    svs = []
    for cidx in range(SUB_MERGE // A_CHUNK):
        r = jnp.dot(ws, v[cidx * A_CHUNK:(cidx + 1) * A_CHUNK, :], preferred_element_type=F32)
        sv = jnp.where(lane < 64, r[0:128],
                       jnp.where(lane < 128, r[128:256], jnp.where(lane < 192, r[256:384], r[384:512])))
        svs.append(sv + bs_ref[...])
    out_a = (u * jnp.concatenate(svs, axis=0)).astype(BF16)

    z = c3_ref[rows, MIX:2 * MIX] * c3_ref[rows, 2 * MIX:3 * MIX]
    if r0 == 0:
        zp = c3p_ref[:, MIX:2 * MIX] * c3p_ref[:, 2 * MIX:3 * MIX]
        zp = jnp.where(first_in_seq, 0.0, zp)
    else:
        zp = c3_ref[r0 - 8:r0, MIX:2 * MIX] * c3_ref[r0 - 8:r0, 2 * MIX:3 * MIX]
    zm1, zm2 = zp[7:8, :], zp[6:7, :]
    rowi = lax.broadcasted_iota(jnp.int32, z.shape, 0)
    z1 = jnp.where(rowi == 0, zm1, pltpu.roll(z, 1, axis=0))
    z2 = jnp.where(rowi == 0, zm2, jnp.where(rowi == 1, zm1, pltpu.roll(z, 2, axis=0)))
    conv = wc_ref[0:1, :] * z2 + wc_ref[1:2, :] * z1 + wc_ref[2:3, :] * z
    out_c = (c3_ref[rows, 0:MIX] * conv).astype(BF16)

    def gate(bi):
        logits = jnp.dot(h, wg_ref[:, bi * D_MODEL:(bi + 1) * D_MODEL], preferred_element_type=F32)
        return 1.0 / (1.0 + jnp.exp(-logits))

    merged = gate(1) * y_b + gate(3) * y_d
    merged = merged + gate(0) * jnp.dot(out_a, wb_ref[0], preferred_element_type=F32)
    merged = merged + gate(2) * jnp.dot(out_c, wb_ref[2], preferred_element_type=F32)
    y = jnp.dot(merged.astype(BF16), wo_ref[...], preferred_element_type=F32)
    o_ref[rows, :] = x + _rms(y, gp_ref[...])


def _merge_kernel(x_ref, auv_ref, c3_ref, c3p_ref, ob_ref, od_ref,
                  g_ref, wg_ref, lng_ref, lnb_ref, ws_ref, bs_ref, wc_ref, wb_ref, wo_ref, gp_ref,
                  o_ref, *, steps_per_seq):
    wrow = lax.broadcasted_iota(jnp.int32, ws_ref.shape, 0) & (A_CHUNK - 1)
    wcol = lax.broadcasted_iota(jnp.int32, ws_ref.shape, 1)
    ws = jnp.where(wrow >= wcol, ws_ref[...], 0.0).astype(BF16)
    first_in_seq = pl.program_id(0) % steps_per_seq == 0
    for r0 in range(0, x_ref.shape[0], SUB_MERGE):
        _merge_rows(r0, first_in_seq, x_ref, auv_ref, c3_ref, c3p_ref, ob_ref, od_ref,
                    g_ref, wg_ref, lng_ref, lnb_ref, ws, bs_ref, wc_ref, wb_ref, wo_ref, gp_ref, o_ref)


def _merge(x2, auv, c3, ob, od, l, p, seq):
    t = x2.shape[0]
    tm = TM_MERGE
    rows = lambda w: pl.BlockSpec((tm, w), lambda i: (i, 0))
    prev8 = pl.BlockSpec((8, 768), lambda i: (jnp.maximum(i * (tm // 8) - 1, 0), 0))
    params = (p["g_pre_mix"], p["w_gate"], p["a_ln_g"], p["a_ln_b"], p["a_w_s"], p["a_b_s"],
              p["c_w_conv"], p["w_branch"], p["w_out"], p["g_post_mix"])
    in_specs = [rows(D_MODEL), rows(512), rows(768), prev8, rows(MIX), rows(MIX)]
    in_specs += [_layer_spec(a, l) for a in params]
    return pl.pallas_call(
        functools.partial(_merge_kernel, steps_per_seq=seq // tm), grid=(t // tm,),
        in_specs=in_specs, out_specs=rows(D_MODEL),
        out_shape=jax.ShapeDtypeStruct((t, D_MODEL), F32),
        compiler_params=pltpu.CompilerParams(dimension_semantics=("arbitrary",),
                                             vmem_limit_bytes=VMEM_LIMIT),
        name="merge",
    )(x2, auv, c3, c3, ob, od, *params)


def _mlp_kernel(x_ref, g_ref, w1_ref, w2_ref, gp_ref, o_ref):
    d_ff = w1_ref.shape[1]
    for r0 in range(0, x_ref.shape[0], SUB_MLP):
        rows = slice(r0, r0 + SUB_MLP)
        x = x_ref[rows, :]
        h = _rms(x, g_ref[...]).astype(BF16)
        y = None
        for cidx in range(d_ff // D_MODEL):
            sl = slice(cidx * D_MODEL, (cidx + 1) * D_MODEL)
            a = jnp.maximum(jnp.dot(h, w1_ref[:, sl], preferred_element_type=F32), 0.0)
            part = jnp.dot((a * a).astype(BF16), w2_ref[sl, :], preferred_element_type=F32)
            y = part if y is None else y + part
        o_ref[rows, :] = x + _rms(y, gp_ref[...])


def _mlp(x2, l, p):
    t = x2.shape[0]
    tm = TM_MLP
    rows = pl.BlockSpec((tm, D_MODEL), lambda i: (i, 0))
    params = (p["g_pre_mlp"], p["w_mlp_in"], p["w_mlp_out"], p["g_post_mlp"])
    return pl.pallas_call(
        _mlp_kernel, grid=(t // tm,),
        in_specs=[rows] + [_layer_spec(a, l) for a in params],
        out_specs=rows, out_shape=jax.ShapeDtypeStruct((t, D_MODEL), F32),
        compiler_params=pltpu.CompilerParams(dimension_semantics=("arbitrary",),
                                             vmem_limit_bytes=VMEM_LIMIT),
        name="mlp",
    )(x2, *params)


def _rope_tables(dim, seq):
    inv = 1.0 / (ROPE_THETA ** (jnp.arange(0, dim, 2, dtype=F32) / dim))
    ang = jnp.arange(seq, dtype=F32)[:, None] * inv[None, :]
    return jnp.cos(ang), jnp.sin(ang)


def _position_tables(seq):
    cos_b, sin_b = _rope_tables(B_ROT, seq)
    cos_d, sin_d = _rope_tables(D_ROPE, seq)
    one, zero = jnp.ones((seq, 1), F32), jnp.zeros((seq, 1), F32)
    cb = jnp.concatenate([jnp.tile(cos_b, (1, N_HEADS)), jnp.tile(one, (1, 96))], axis=1)
    sb = jnp.concatenate([jnp.tile(sin_b, (1, N_HEADS)), jnp.tile(zero, (1, 96))], axis=1)
    cq = jnp.concatenate([jnp.tile(one, (1, 64)), cos_d, cos_d, jnp.tile(zero, (1, 32))], axis=1)
    sq = jnp.concatenate([jnp.tile(zero, (1, 64)), sin_d, sin_d, jnp.tile(zero, (1, 32))], axis=1)
    ck = jnp.concatenate([jnp.tile(zero, (1, 64)), cos_d, cos_d, jnp.tile(zero, (1, 32))], axis=1)
    return jnp.concatenate([cb, sb, cq, sq, ck], axis=1)


def _pack_params(g_pre_mix, w_in, a_ln_g, a_ln_b, a_w_s, a_b_s, c_w_conv, d_q_norm_g, d_w_uq, d_kv_norm_g,
                 d_w_ukv, w_branch, w_out, g_post_mix, g_pre_mlp, w_mlp_in, w_mlp_out, g_post_mlp):
    depth = w_in.shape[0]
    z = lambda n: jnp.zeros((depth, D_MODEL, n), w_in.dtype)
    a = w_in[..., 0:512]
    bq, bk, bv = w_in[..., 512:768], w_in[..., 768:1024], w_in[..., 1024:1280]
    c = w_in[..., 1280:2048]
    cq, ckv, kr = w_in[..., 2048:2240], w_in[..., 2240:2368], w_in[..., 2368:2400]
    kr_sw = jnp.concatenate([-kr[..., 16:32], kr[..., 0:16]], axis=-1)
    w1 = jnp.concatenate([a, _moba_cols(bq), _moba_cols(bk), bv, c,
                          cq[..., 0:128], cq[..., 128:192], kr, z(32),
                          z(64), kr_sw, z(32), ckv], axis=-1).astype(BF16)

    nope, x1, x2 = d_w_uq[..., 0:64], d_w_uq[..., 64:80], d_w_uq[..., 80:96]
    zq = lambda n: jnp.zeros((depth, D_Q_LORA, N_HEADS, n), d_w_uq.dtype)
    main = jnp.concatenate([nope, x1, x2, zq(32)], axis=-1).reshape(depth, D_Q_LORA, 512)
    swap = jnp.concatenate([zq(64), -x2, x1, zq(32)], axis=-1).reshape(depth, D_Q_LORA, 512)
    wq = jnp.concatenate([main, swap], axis=-1)
    wq = jnp.concatenate([wq, jnp.zeros((depth, 256 - D_Q_LORA, 1024), wq.dtype)], axis=1).astype(BF16)
    gq = jnp.concatenate([d_q_norm_g, jnp.zeros((depth, 256 - D_Q_LORA), F32)], axis=-1)

    wk = jnp.concatenate([d_w_ukv[..., 0:64], jnp.zeros((depth, D_KV_LORA, N_HEADS, 64), d_w_ukv.dtype)],
                         axis=-1).reshape(depth, D_KV_LORA, 512)
    wv = d_w_ukv[..., 64:128].reshape(depth, D_KV_LORA, MIX)
    row = lambda g: g[:, None, :]
    return {
        "g_pre_mix": row(g_pre_mix), "w1": w1, "gq": row(gq), "gkv": row(d_kv_norm_g), "wq": wq,
        "wkv": jnp.concatenate([wk, wv], axis=-1).astype(BF16),
        "w_gate": w_in[..., 2400:].astype(BF16),
        "a_ln_g": row(a_ln_g), "a_ln_b": row(a_ln_b),
        "a_w_s": a_w_s.reshape(depth, N_HEADS * A_CHUNK, A_CHUNK),
        "a_b_s": jnp.repeat(jnp.swapaxes(a_b_s, 1, 2), HEAD_DIM, axis=2),
        "c_w_conv": c_w_conv, "w_branch": w_branch.astype(BF16), "w_out": w_out.astype(BF16),
        "g_post_mix": row(g_post_mix), "g_pre_mlp": row(g_pre_mlp),
        "w_mlp_in": w_mlp_in.astype(BF16), "w_mlp_out": w_mlp_out.astype(BF16),
        "g_post_mlp": row(g_post_mlp),
    }


def kernel(x, g_pre_mix, w_in, a_ln_g, a_ln_b, a_w_s, a_b_s, c_w_conv, d_q_norm_g, d_w_uq,
           d_kv_norm_g, d_w_ukv, w_branch, w_out, g_post_mix, g_pre_mlp, w_mlp_in, w_mlp_out,
           g_post_mlp):
    batch, seq, _ = x.shape
    t = batch * seq
    tab = _position_tables(seq)
    p = _pack_params(g_pre_mix, w_in, a_ln_g, a_ln_b, a_w_s, a_b_s, c_w_conv, d_q_norm_g, d_w_uq,
                     d_kv_norm_g, d_w_ukv, w_branch, w_out, g_post_mix, g_pre_mlp, w_mlp_in,
                     w_mlp_out, g_post_mlp)
    x2 = x.reshape(t, D_MODEL)
    for l in range(w_in.shape[0]):
        (auv, bqh, bql, bk, bvt, km, c3, dqt, dk, dvt) = _inproj(x2, l, p, tab, seq)
        km = km.reshape(batch, seq // MOBA_BLOCK, MIX)
        out_b = _moba(bqh, bql, bk, bvt, km, batch, seq)
        out_d = _mla(dqt, dk, dvt, batch, seq)
        x2 = _merge(x2, auv, c3, out_b, out_d, l, p, seq)
        x2 = _mlp(x2, l, p)
    return x2.reshape(batch, seq, D_MODEL)
```

```python
import functools
import math

import jax
import jax.numpy as jnp
from jax import lax
from jax.experimental import pallas as pl
from jax.experimental.pallas import tpu as pltpu

F32 = jnp.float32
BF16 = jnp.bfloat16

D_MODEL = 1024
HEAD_DIM = 64
N_HEADS = 4
MIX = 256
A_CHUNK = 128
MOBA_BLOCK = 256
MOBA_TOPK = 3
B_ROT = 16
D_NOPE, D_ROPE, D_Q_LORA, D_KV_LORA = 64, 32, 192, 128
ROPE_THETA = 500000.0
NORM_EPS = 1e-6
LN_EPS = 1e-5
LOG2E = 1.4426950408889634
MOBA_QSCALE = (HEAD_DIM ** -0.5) * LOG2E
MLA_QSCALE = ((D_NOPE + D_ROPE) ** -0.5) * LOG2E

LANE = 128
TM_IN = 1024
SUB_IN = 512
TM_MERGE = 512
SUB_MERGE = 256
TM_MLP = 1024
SUB_MLP = 512
TQ = 256
TK = 512
L_ROWS = 16
PIPE_DEPTH = 4
S_SLOTS = PIPE_DEPTH + 2
VMEM_LIMIT = 56 * 1024 * 1024

N1 = 2560
NEG_INF = float("-inf")


def _moba_cols(w):
    lead = w.shape[:-1]
    w4 = w.reshape(lead + (N_HEADS, HEAD_DIM))
    part = lambda a, b: w4[..., a:b].reshape(lead + (N_HEADS * (b - a),))
    return jnp.concatenate([part(0, 8), part(16, 40), part(8, 16), part(40, 64)], axis=-1)


def _moba_head_mask(idx, h):
    l = idx & (LANE - 1)
    return ((l >= 8 * h) & (l < 8 * h + 8)) | ((l >= 32 + 24 * h) & (l < 56 + 24 * h))


def _rms(x, g):
    return x * lax.rsqrt(jnp.mean(x * x, axis=-1, keepdims=True) + NORM_EPS) * g


def _gelu(x):
    return 0.5 * x * (1.0 + jnp.tanh(math.sqrt(2.0 / math.pi) * (x + 0.044715 * (x * x * x))))


def _layer_spec(arr, l):
    nd = arr.ndim - 1
    return pl.BlockSpec((None,) + arr.shape[1:], lambda *_: (l,) + (0,) * nd, pipeline_mode=pl.Buffered(1))


def _inproj_rows(blk, x_ref, tab_ref, g_ref, w1_ref, gq_ref, gkv_ref, wq_ref, wkv_ref,
                 auv_ref, bqh_ref, bql_ref, bk_ref, bvt_ref, km_ref, c3_ref, dqt_ref, dk_ref, dvt_ref):
    tm = SUB_IN
    rows = slice(blk * tm, (blk + 1) * tm)
    h = _rms(x_ref[rows, :], g_ref[...]).astype(BF16)

    def proj(a, b):
        return jnp.dot(h, w1_ref[:, a:b], preferred_element_type=F32)

    d = proj(2048, 2560)
    cq = d[:, 0:256]
    lane = lax.broadcasted_iota(jnp.int32, cq.shape, 1)
    ms = jnp.sum(jnp.where(lane < D_Q_LORA, cq * cq, 0.0), axis=-1, keepdims=True) * (1.0 / D_Q_LORA)
    cqn = (cq * lax.rsqrt(ms + NORM_EPS) * gq_ref[...]).astype(BF16)
    qq = jnp.dot(cqn, wq_ref[...], preferred_element_type=F32)
    cq_t = jnp.concatenate([tab_ref[rows, 256:384]] * N_HEADS, axis=1)
    sq_t = jnp.concatenate([tab_ref[rows, 384:512]] * N_HEADS, axis=1)
    qd = (qq[:, 0:512] * cq_t + qq[:, 512:1024] * sq_t) * MLA_QSCALE
    qdt = qd.T.astype(BF16)

    kvn = _rms(d[:, 384:512], gkv_ref[...]).astype(BF16)
    kv = jnp.dot(kvn, wkv_ref[...], preferred_element_type=F32)
    kr = d[:, 128:256] * tab_ref[rows, 512:640] + d[:, 256:384] * tab_ref[rows, 384:512]
    dk_ref[rows, :] = (kv[:, 0:512] + jnp.concatenate([kr] * N_HEADS, axis=1)).astype(BF16)
    vdt = kv[:, 512:768].T.astype(BF16)
    for j in range(tm // TQ):
        dqt_ref[blk * (tm // TQ) + j] = qdt[:, j * TQ:(j + 1) * TQ]
    for j in range(tm // TK):
        dvt_ref[blk * (tm // TK) + j] = vdt[:, j * TK:(j + 1) * TK]

    cb = tab_ref[rows, 0:128]
    sb = tab_ref[rows, 128:256]

    def rope(y):
        y0, y1 = y[:, 0:LANE], y[:, LANE:2 * LANE]
        return jnp.concatenate([y0 * cb - y1 * sb, y1 * cb + y0 * sb], axis=1)

    q = rope(proj(512, 768)) * MOBA_QSCALE
    k = rope(proj(768, 1024))
    v = proj(1024, 1280)
    qt = q.T
    qt_hi = qt.astype(BF16)
    qt_lo = (qt - qt_hi.astype(F32)).astype(BF16)
    vt = v.T.astype(BF16)
    bk_ref[rows, :] = k.astype(BF16)
    for j in range(tm // TQ):
        bqh_ref[blk * (tm // TQ) + j] = qt_hi[:, j * TQ:(j + 1) * TQ]
        bql_ref[blk * (tm // TQ) + j] = qt_lo[:, j * TQ:(j + 1) * TQ]
    for j in range(tm // TK):
        bvt_ref[blk * (tm // TK) + j] = vt[:, j * TK:(j + 1) * TK]
    for j in range(tm // MOBA_BLOCK):
        kb = k[j * MOBA_BLOCK:(j + 1) * MOBA_BLOCK, :]
        jj = blk * (tm // MOBA_BLOCK) + j
        km_ref[0, jj:jj + 1, :] = jnp.sum(kb, axis=0, keepdims=True) * (1.0 / MOBA_BLOCK)

    auv_ref[rows, :] = proj(0, 512)
    c3_ref[rows, :] = proj(1280, 2048)


def _inproj_kernel(x_ref, *refs):
    for blk in range(x_ref.shape[0] // SUB_IN):
        _inproj_rows(blk, x_ref, *refs)


def _inproj(x2, l, p, tab, seq):
    t = x2.shape[0]
    tm = TM_IN
    nt = t // tm
    spt = seq // tm
    rows = lambda w: pl.BlockSpec((tm, w), lambda i: (i, 0))
    tblk = lambda r: pl.BlockSpec((tm // TQ, r, TQ), lambda i: (i, 0, 0))
    vblk = pl.BlockSpec((tm // TK, MIX, TK), lambda i: (i, 0, 0))
    out_shape = (
        jax.ShapeDtypeStruct((t, 512), F32),
        jax.ShapeDtypeStruct((t // TQ, MIX, TQ), BF16),
        jax.ShapeDtypeStruct((t // TQ, MIX, TQ), BF16),
        jax.ShapeDtypeStruct((t, MIX), BF16),
        jax.ShapeDtypeStruct((t // TK, MIX, TK), BF16),
        jax.ShapeDtypeStruct((nt, tm // MOBA_BLOCK, MIX), F32),
        jax.ShapeDtypeStruct((t, 768), F32),
        jax.ShapeDtypeStruct((t // TQ, 512, TQ), BF16),
        jax.ShapeDtypeStruct((t, 512), BF16),
        jax.ShapeDtypeStruct((t // TK, MIX, TK), BF16),
    )
    out_specs = (
        rows(512), tblk(MIX), tblk(MIX), rows(MIX), vblk,
        pl.BlockSpec((1, tm // MOBA_BLOCK, MIX), lambda i: (i, 0, 0)),
        rows(768), tblk(512), rows(512), vblk,
    )
    params = (p["g_pre_mix"], p["w1"], p["gq"], p["gkv"], p["wq"], p["wkv"])
    in_specs = [rows(D_MODEL), pl.BlockSpec((tm, 640), lambda i: (i % spt, 0))]
    in_specs += [_layer_spec(a, l) for a in params]
    return pl.pallas_call(
        _inproj_kernel, grid=(nt,), in_specs=in_specs, out_specs=out_specs, out_shape=out_shape,
        compiler_params=pltpu.CompilerParams(dimension_semantics=("arbitrary",),
                                             vmem_limit_bytes=VMEM_LIMIT),
        name="inproj",
    )(x2, tab, *params)


def _attend(make_scores, vt_ref, s_ref, o_ref, k_windows):
    hs = [slice(h * HEAD_DIM, (h + 1) * HEAD_DIM) for h in range(N_HEADS)]
    row = lax.broadcasted_iota(jnp.int32, (MOBA_BLOCK, TQ), 0)
    col = lax.broadcasted_iota(jnp.int32, (MOBA_BLOCK, TQ), 1)
    causal = row <= col
    ones = jnp.ones((L_ROWS, TK), BF16)
    units = [(sub, h) for sub in range(2) for h in range(N_HEADS)]
    n_slots = s_ref.shape[0]

    def body(k):
        scorers = [make_scores(sub) for sub in range(2)]
        m, acc, out = {}, {}, {}

        def keys_in(sub, w):
            return TK if w < k else (sub + 1) * MOBA_BLOCK

        def score_step(t, u, w):
            sub, h = units[u]
            n = keys_in(sub, w)
            s, bias = scorers[sub](w, h, n)
            if w == k:
                own = jnp.where(causal, s[n - MOBA_BLOCK:n], NEG_INF)
                s = own if n == MOBA_BLOCK else jnp.concatenate([s[0:n - MOBA_BLOCK], own], axis=0)
            s_ref[t % n_slots, 0:n] = s
            mw = None
            for j in range(n // MOBA_BLOCK):
                mb = jnp.max(s[j * MOBA_BLOCK:(j + 1) * MOBA_BLOCK], axis=0, keepdims=True)
                mb = mb if bias is None else mb + bias[j]
                mw = mb if mw is None else jnp.maximum(mw, mb)
            return mw, bias

        def value_step(t, u, w, mw_bias):
            sub, h = units[u]
            n = keys_in(sub, w)
            mw, bias = mw_bias
            m_new = mw if u not in m else jnp.maximum(m[u], mw)
            if bias is None:
                p = jnp.exp2(s_ref[t % n_slots, 0:n] - m_new)
            else:
                p = jnp.concatenate(
                    [jnp.exp2(s_ref[t % n_slots, j * MOBA_BLOCK:(j + 1) * MOBA_BLOCK] - (m_new - bias[j]))
                     for j in range(n // MOBA_BLOCK)], axis=0)
            p = p.astype(BF16)
            v_aug = jnp.concatenate([vt_ref[w, hs[h], 0:n], ones[:, 0:n]], axis=0)
            r = jnp.dot(v_aug, p, preferred_element_type=F32)
            acc[u] = r if u not in acc else jnp.exp2(m[u] - m_new) * acc[u] + r
            m[u] = m_new
            if w == last_window:
                out[u] = acc[u][0:HEAD_DIM, :] * (1.0 / acc[u][HEAD_DIM:HEAD_DIM + 1, :])

        windows = [k] + list(range(k))
        last_window = windows[-1]
        steps = [(u, w) for u in range(len(units)) for w in windows]
        queue = []
        for t, (u, w) in enumerate(steps):
            queue.append((t, u, w, score_step(t, u, w)))
            if len(queue) > PIPE_DEPTH:
                value_step(*queue.pop(0))
        while queue:
            value_step(*queue.pop(0))
        for sub in range(2):
            tile = jnp.concatenate([out[sub * N_HEADS + h] for h in range(N_HEADS)], axis=0).T
            o_ref[sub * TQ:(sub + 1) * TQ, :] = tile.astype(o_ref.dtype)

    for k in range(vt_ref.shape[0]):
        pl.when(k_windows == k)(functools.partial(body, k))


def _moba_kernel(qh_ref, ql_ref, k_ref, vt_ref, km_ref, o_ref, qm_s, s_ref):
    i2 = pl.program_id(1)
    row = lax.broadcasted_iota(jnp.int32, (MIX, TQ), 0)
    for sub in range(2):
        qt_hi = qh_ref[sub]
        for h in range(N_HEADS):
            qm_s[sub, h] = jnp.where(_moba_head_mask(row, h), qt_hi, jnp.zeros_like(qt_hi))

    def make_scores(sub):
        i = 2 * i2 + sub
        km = km_ref[0]
        nb = km.shape[0]
        lane = lax.broadcasted_iota(jnp.int32, km.shape, 1)
        kmh = jnp.concatenate([jnp.where(_moba_head_mask(lane, h), km, 0.0) for h in range(N_HEADS)], axis=0)
        km_hi = kmh.astype(BF16)
        km_lo = (kmh - km_hi.astype(F32)).astype(BF16)
        gate = (jnp.dot(km_hi, qh_ref[sub], preferred_element_type=F32)
                + jnp.dot(km_hi, ql_ref[sub], preferred_element_type=F32)
                + jnp.dot(km_lo, qh_ref[sub], preferred_element_type=F32))

        blk = lax.broadcasted_iota(jnp.int32, (nb, TQ), 0)
        blkf = blk.astype(F32)
        biases = []
        for h in range(N_HEADS):
            g = jnp.where(blk < i, gate[h * nb:(h + 1) * nb, :], NEG_INF)
            bias = jnp.full((nb, TQ), NEG_INF, F32)
            for _ in range(MOBA_TOPK):
                mx = jnp.max(g, axis=0, keepdims=True)
                first = jnp.min(jnp.where(g == mx, blkf, float(nb)), axis=0, keepdims=True)
                hit = blkf == first
                bias = jnp.where(hit & (mx > NEG_INF), 0.0, bias)
                g = jnp.where(hit, NEG_INF, g)
            biases.append(jnp.where(blk == i, 0.0, bias))

        def scores(w, h, n):
            s = jnp.dot(k_ref[w * TK:w * TK + n, :], qm_s[sub, h], preferred_element_type=F32)
            b = biases[h]
            return s, [b[2 * w + j:2 * w + j + 1, :] for j in range(n // MOBA_BLOCK)]

        return scores

    _attend(make_scores, vt_ref, s_ref, o_ref, i2)


def _moba(qh, ql, k, vt, km, batch, seq):
    nq2 = seq // (2 * TQ)
    t = batch * seq
    qspec = pl.BlockSpec((2, MIX, TQ), lambda b, i: (b * nq2 + i, 0, 0))
    return pl.pallas_call(
        _moba_kernel, grid=(batch, nq2),
        in_specs=[qspec, qspec,
                  pl.BlockSpec((seq, MIX), lambda b, i: (b, 0)),
                  pl.BlockSpec((seq // TK, MIX, TK), lambda b, i: (b, 0, 0)),
                  pl.BlockSpec((1, seq // MOBA_BLOCK, MIX), lambda b, i: (b, 0, 0))],
        out_specs=pl.BlockSpec((2 * TQ, MIX), lambda b, i: (b * nq2 + i, 0)),
        out_shape=jax.ShapeDtypeStruct((t, MIX), BF16),
        scratch_shapes=[pltpu.VMEM((2, N_HEADS, MIX, TQ), BF16),
                        pltpu.VMEM((S_SLOTS, TK, TQ), F32)],
        compiler_params=pltpu.CompilerParams(dimension_semantics=("arbitrary", "arbitrary"),
                                             vmem_limit_bytes=VMEM_LIMIT),
        name="moba",
    )(qh, ql, k, vt, km)


def _mla_kernel(qt_ref, k_ref, vt_ref, o_ref, s_ref):
    def make_scores(sub):
        def scores(w, h, n):
            kw = k_ref[w * TK:w * TK + n, h * LANE:(h + 1) * LANE]
            return jnp.dot(kw, qt_ref[sub, h * LANE:(h + 1) * LANE, :], preferred_element_type=F32), None
        return scores

    _attend(make_scores, vt_ref, s_ref, o_ref, pl.program_id(1))


def _mla(qt, k, vt, batch, seq):
    nq2 = seq // (2 * TQ)
    t = batch * seq
    return pl.pallas_call(
        _mla_kernel, grid=(batch, nq2),
        in_specs=[pl.BlockSpec((2, 512, TQ), lambda b, i: (b * nq2 + i, 0, 0)),
                  pl.BlockSpec((seq, 512), lambda b, i: (b, 0)),
                  pl.BlockSpec((seq // TK, MIX, TK), lambda b, i: (b, 0, 0))],
        out_specs=pl.BlockSpec((2 * TQ, MIX), lambda b, i: (b * nq2 + i, 0)),
        out_shape=jax.ShapeDtypeStruct((t, MIX), BF16),
        scratch_shapes=[pltpu.VMEM((S_SLOTS, TK, TQ), F32)],
        compiler_params=pltpu.CompilerParams(dimension_semantics=("arbitrary", "arbitrary"),
                                             vmem_limit_bytes=VMEM_LIMIT),
        name="mla",
    )(qt, k, vt)


def _merge_rows(r0, first_in_seq, x_ref, auv_ref, c3_ref, c3p_ref, ob_ref, od_ref,
                g_ref, wg_ref, lng_ref, lnb_ref, ws, bs_ref, wc_ref, wb_ref, wo_ref, gp_ref, o_ref):
    rows = slice(r0, r0 + SUB_MERGE)
    y_b = jnp.dot(ob_ref[rows, :], wb_ref[1], preferred_element_type=F32)
    y_d = jnp.dot(od_ref[rows, :], wb_ref[3], preferred_element_type=F32)
    x = x_ref[rows, :]
    h = _rms(x, g_ref[...]).astype(BF16)

    u = _gelu(auv_ref[rows, 0:MIX])
    v = _gelu(auv_ref[rows, MIX:2 * MIX])
    mu = jnp.mean(v, axis=-1, keepdims=True)
    vc = v - mu
    var = jnp.mean(vc * vc, axis=-1, keepdims=True)
    v = (vc * lax.rsqrt(var + LN_EPS) * lng_ref[...] + lnb_ref[...]).astype(BF16)
    lane = lax.broadcasted_iota(jnp.int32, (A_CHUNK, MIX), 1)
    svs = []
    for cidx in range(SUB_MERGE // A_CHUNK):
        r = jnp.dot(ws, v[cidx * A_CHUNK:(cidx + 1) * A_CHUNK, :], preferred_element_type=F32)
        sv = jnp.where(lane < 64, r[0:128],
                       jnp.where(lane < 128, r[128:256], jnp.where(lane < 192, r[256:384], r[384:512])))
        svs.append(sv + bs_ref[...])
    out_a = (u * jnp.concatenate(svs, axis=0)).astype(BF16)

    z = c3_ref[rows, MIX:2 * MIX] * c3_ref[rows, 2 * MIX:3 * MIX]
    if r0 == 0:
        zp = c3p_ref[:, MIX:2 * MIX] * c3p_ref[:, 2 * MIX:3 * MIX]
        zp = jnp.where(first_in_seq, 0.0, zp)
    else:
        zp = c3_ref[r0 - 8:r0, MIX:2 * MIX] * c3_ref[r0 - 8:r0, 2 * MIX:3 * MIX]
    zm1, zm2 = zp[7:8, :], zp[6:7, :]
    rowi = lax.broadcasted_iota(jnp.int32, z.shape, 0)
    z1 = jnp.where(rowi == 0, zm1, pltpu.roll(z, 1, axis=0))
    z2 = jnp.where(rowi == 0, zm2, jnp.where(rowi == 1, zm1, pltpu.roll(z, 2, axis=0)))
    conv = wc_ref[0:1, :] * z2 + wc_ref[1:2, :] * z1 + wc_ref[2:3, :] * z
    out_c = (c3_ref[rows, 0:MIX] * conv).astype(BF16)

    def gate(bi):
        logits = jnp.dot(h, wg_ref[:, bi * D_MODEL:(bi + 1) * D_MODEL], preferred_element_type=F32)
        return 1.0 / (1.0 + jnp.exp(-logits))

    merged = gate(1) * y_b + gate(3) * y_d
    merged = merged + gate(0) * jnp.dot(out_a, wb_ref[0], preferred_element_type=F32)
    merged = merged + gate(2) * jnp.dot(out_c, wb_ref[2], preferred_element_type=F32)
    y = jnp.dot(merged.astype(BF16), wo_ref[...], preferred_element_type=F32)
    o_ref[rows, :] = x + _rms(y, gp_ref[...])


def _merge_kernel(x_ref, auv_ref, c3_ref, c3p_ref, ob_ref, od_ref,
                  g_ref, wg_ref, lng_ref, lnb_ref, ws_ref, bs_ref, wc_ref, wb_ref, wo_ref, gp_ref,
                  o_ref, *, steps_per_seq):
    wrow = lax.broadcasted_iota(jnp.int32, ws_ref.shape, 0) & (A_CHUNK - 1)
    wcol = lax.broadcasted_iota(jnp.int32, ws_ref.shape, 1)
    ws = jnp.where(wrow >= wcol, ws_ref[...], 0.0).astype(BF16)
    first_in_seq = pl.program_id(0) % steps_per_seq == 0
    for r0 in range(0, x_ref.shape[0], SUB_MERGE):
        _merge_rows(r0, first_in_seq, x_ref, auv_ref, c3_ref, c3p_ref, ob_ref, od_ref,
                    g_ref, wg_ref, lng_ref, lnb_ref, ws, bs_ref, wc_ref, wb_ref, wo_ref, gp_ref, o_ref)


def _merge(x2, auv, c3, ob, od, l, p, seq):
    t = x2.shape[0]
    tm = TM_MERGE
    rows = lambda w: pl.BlockSpec((tm, w), lambda i: (i, 0))
    prev8 = pl.BlockSpec((8, 768), lambda i: (jnp.maximum(i * (tm // 8) - 1, 0), 0))
    params = (p["g_pre_mix"], p["w_gate"], p["a_ln_g"], p["a_ln_b"], p["a_w_s"], p["a_b_s"],
              p["c_w_conv"], p["w_branch"], p["w_out"], p["g_post_mix"])
    in_specs = [rows(D_MODEL), rows(512), rows(768), prev8, rows(MIX), rows(MIX)]
    in_specs += [_layer_spec(a, l) for a in params]
    return pl.pallas_call(
        functools.partial(_merge_kernel, steps_per_seq=seq // tm), grid=(t // tm,),
        in_specs=in_specs, out_specs=rows(D_MODEL),
        out_shape=jax.ShapeDtypeStruct((t, D_MODEL), F32),
        compiler_params=pltpu.CompilerParams(dimension_semantics=("arbitrary",),
                                             vmem_limit_bytes=VMEM_LIMIT),
        name="merge",
    )(x2, auv, c3, c3, ob, od, *params)


def _mlp_kernel(x_ref, g_ref, w1_ref, w2_ref, gp_ref, o_ref):
    d_ff = w1_ref.shape[1]
    for r0 in range(0, x_ref.shape[0], SUB_MLP):
        rows = slice(r0, r0 + SUB_MLP)
        x = x_ref[rows, :]
        h = _rms(x, g_ref[...]).astype(BF16)
        y = None
        for cidx in range(d_ff // D_MODEL):
            sl = slice(cidx * D_MODEL, (cidx + 1) * D_MODEL)
            a = jnp.maximum(jnp.dot(h, w1_ref[:, sl], preferred_element_type=F32), 0.0)
            part = jnp.dot((a * a).astype(BF16), w2_ref[sl, :], preferred_element_type=F32)
            y = part if y is None else y + part
        o_ref[rows, :] = x + _rms(y, gp_ref[...])


def _mlp(x2, l, p):
    t = x2.shape[0]
    tm = TM_MLP
    rows = pl.BlockSpec((tm, D_MODEL), lambda i: (i, 0))
    params = (p["g_pre_mlp"], p["w_mlp_in"], p["w_mlp_out"], p["g_post_mlp"])
    return pl.pallas_call(
        _mlp_kernel, grid=(t // tm,),
        in_specs=[rows] + [_layer_spec(a, l) for a in params],
        out_specs=rows, out_shape=jax.ShapeDtypeStruct((t, D_MODEL), F32),
        compiler_params=pltpu.CompilerParams(dimension_semantics=("arbitrary",),
                                             vmem_limit_bytes=VMEM_LIMIT),
        name="mlp",
    )(x2, *params)


def _rope_tables(dim, seq):
    inv = 1.0 / (ROPE_THETA ** (jnp.arange(0, dim, 2, dtype=F32) / dim))
    ang = jnp.arange(seq, dtype=F32)[:, None] * inv[None, :]
    return jnp.cos(ang), jnp.sin(ang)


def _position_tables(seq):
    cos_b, sin_b = _rope_tables(B_ROT, seq)
    cos_d, sin_d = _rope_tables(D_ROPE, seq)
    one, zero = jnp.ones((seq, 1), F32), jnp.zeros((seq, 1), F32)
    cb = jnp.concatenate([jnp.tile(cos_b, (1, N_HEADS)), jnp.tile(one, (1, 96))], axis=1)
    sb = jnp.concatenate([jnp.tile(sin_b, (1, N_HEADS)), jnp.tile(zero, (1, 96))], axis=1)
    cq = jnp.concatenate([jnp.tile(one, (1, 64)), cos_d, cos_d, jnp.tile(zero, (1, 32))], axis=1)
    sq = jnp.concatenate([jnp.tile(zero, (1, 64)), sin_d, sin_d, jnp.tile(zero, (1, 32))], axis=1)
    ck = jnp.concatenate([jnp.tile(zero, (1, 64)), cos_d, cos_d, jnp.tile(zero, (1, 32))], axis=1)
    return jnp.concatenate([cb, sb, cq, sq, ck], axis=1)


def _pack_params(g_pre_mix, w_in, a_ln_g, a_ln_b, a_w_s, a_b_s, c_w_conv, d_q_norm_g, d_w_uq, d_kv_norm_g,
                 d_w_ukv, w_branch, w_out, g_post_mix, g_pre_mlp, w_mlp_in, w_mlp_out, g_post_mlp):
    depth = w_in.shape[0]
    z = lambda n: jnp.zeros((depth, D_MODEL, n), w_in.dtype)
    a = w_in[..., 0:512]
    bq, bk, bv = w_in[..., 512:768], w_in[..., 768:1024], w_in[..., 1024:1280]
    c = w_in[..., 1280:2048]
    cq, ckv, kr = w_in[..., 2048:2240], w_in[..., 2240:2368], w_in[..., 2368:2400]
    kr_sw = jnp.concatenate([-kr[..., 16:32], kr[..., 0:16]], axis=-1)
    w1 = jnp.concatenate([a, _moba_cols(bq), _moba_cols(bk), bv, c,
                          cq[..., 0:128], cq[..., 128:192], kr, z(32),
                          z(64), kr_sw, z(32), ckv], axis=-1).astype(BF16)

    nope, x1, x2 = d_w_uq[..., 0:64], d_w_uq[..., 64:80], d_w_uq[..., 80:96]
    zq = lambda n: jnp.zeros((depth, D_Q_LORA, N_HEADS, n), d_w_uq.dtype)
    main = jnp.concatenate([nope, x1, x2, zq(32)], axis=-1).reshape(depth, D_Q_LORA, 512)
    swap = jnp.concatenate([zq(64), -x2, x1, zq(32)], axis=-1).reshape(depth, D_Q_LORA, 512)
    wq = jnp.concatenate([main, swap], axis=-1)
    wq = jnp.concatenate([wq, jnp.zeros((depth, 256 - D_Q_LORA, 1024), wq.dtype)], axis=1).astype(BF16)
    gq = jnp.concatenate([d_q_norm_g, jnp.zeros((depth, 256 - D_Q_LORA), F32)], axis=-1)

    wk = jnp.concatenate([d_w_ukv[..., 0:64], jnp.zeros((depth, D_KV_LORA, N_HEADS, 64), d_w_ukv.dtype)],
                         axis=-1).reshape(depth, D_KV_LORA, 512)
    wv = d_w_ukv[..., 64:128].reshape(depth, D_KV_LORA, MIX)
    row = lambda g: g[:, None, :]
    return {
        "g_pre_mix": row(g_pre_mix), "w1": w1, "gq": row(gq), "gkv": row(d_kv_norm_g), "wq": wq,
        "wkv": jnp.concatenate([wk, wv], axis=-1).astype(BF16),
        "w_gate": w_in[..., 2400:].astype(BF16),
        "a_ln_g": row(a_ln_g), "a_ln_b": row(a_ln_b),
        "a_w_s": a_w_s.reshape(depth, N_HEADS * A_CHUNK, A_CHUNK),
        "a_b_s": jnp.repeat(jnp.swapaxes(a_b_s, 1, 2), HEAD_DIM, axis=2),
        "c_w_conv": c_w_conv, "w_branch": w_branch.astype(BF16), "w_out": w_out.astype(BF16),
        "g_post_mix": row(g_post_mix), "g_pre_mlp": row(g_pre_mlp),
        "w_mlp_in": w_mlp_in.astype(BF16), "w_mlp_out": w_mlp_out.astype(BF16),
        "g_post_mlp": row(g_post_mlp),
    }


def kernel(x, g_pre_mix, w_in, a_ln_g, a_ln_b, a_w_s, a_b_s, c_w_conv, d_q_norm_g, d_w_uq,
           d_kv_norm_g, d_w_ukv, w_branch, w_out, g_post_mix, g_pre_mlp, w_mlp_in, w_mlp_out,
           g_post_mlp):
    batch, seq, _ = x.shape
    t = batch * seq
    tab = _position_tables(seq)
    p = _pack_params(g_pre_mix, w_in, a_ln_g, a_ln_b, a_w_s, a_b_s, c_w_conv, d_q_norm_g, d_w_uq,
                     d_kv_norm_g, d_w_ukv, w_branch, w_out, g_post_mix, g_pre_mlp, w_mlp_in,
                     w_mlp_out, g_post_mlp)
    x2 = x.reshape(t, D_MODEL)
    for l in range(w_in.shape[0]):
        (auv, bqh, bql, bk, bvt, km, c3, dqt, dk, dvt) = _inproj(x2, l, p, tab, seq)
        km = km.reshape(batch, seq // MOBA_BLOCK, MIX)
        out_b = _moba(bqh, bql, bk, bvt, km, batch, seq)
        out_d = _mla(dqt, dk, dvt, batch, seq)
        x2 = _merge(x2, auv, c3, out_b, out_d, l, p, seq)
        x2 = _mlp(x2, l, p)
    return x2.reshape(batch, seq, D_MODEL)
```

```python
import functools
import math

import jax
import jax.numpy as jnp
from jax import lax
from jax.experimental import pallas as pl
from jax.experimental.pallas import tpu as pltpu

F32 = jnp.float32
BF16 = jnp.bfloat16

D_MODEL = 1024
HEAD_DIM = 64
N_HEADS = 4
MIX = 256
A_CHUNK = 128
MOBA_BLOCK = 256
MOBA_TOPK = 3
B_ROT = 16
D_NOPE, D_ROPE, D_Q_LORA, D_KV_LORA = 64, 32, 192, 128
ROPE_THETA = 500000.0
NORM_EPS = 1e-6
LN_EPS = 1e-5
LOG2E = 1.4426950408889634
MOBA_QSCALE = (HEAD_DIM ** -0.5) * LOG2E
MLA_QSCALE = ((D_NOPE + D_ROPE) ** -0.5) * LOG2E

LANE = 128
TM_IN = 1024
SUB_IN = 512
TM_MERGE = 1024
SUB_MERGE = 256
TM_MLP = 1024
SUB_MLP = 512
TQ = 256
TK = 512
L_ROWS = 16
PIPE_DEPTH = 4
S_SLOTS = PIPE_DEPTH + 2
VMEM_LIMIT = 56 * 1024 * 1024

N1 = 2560
NEG_INF = float("-inf")


def _moba_cols(w):
    lead = w.shape[:-1]
    w4 = w.reshape(lead + (N_HEADS, HEAD_DIM))
    part = lambda a, b: w4[..., a:b].reshape(lead + (N_HEADS * (b - a),))
    return jnp.concatenate([part(0, 8), part(16, 40), part(8, 16), part(40, 64)], axis=-1)


def _moba_head_mask(idx, h):
    l = idx & (LANE - 1)
    return ((l >= 8 * h) & (l < 8 * h + 8)) | ((l >= 32 + 24 * h) & (l < 56 + 24 * h))


def _rms(x, g):
    return x * lax.rsqrt(jnp.mean(x * x, axis=-1, keepdims=True) + NORM_EPS) * g


def _gelu(x):
    return 0.5 * x * (1.0 + jnp.tanh(math.sqrt(2.0 / math.pi) * (x + 0.044715 * (x * x * x))))


def _layer_spec(arr, l):
    nd = arr.ndim - 1
    return pl.BlockSpec((None,) + arr.shape[1:], lambda *_: (l,) + (0,) * nd, pipeline_mode=pl.Buffered(1))


def _inproj_rows(blk, x_ref, tab_ref, g_ref, w1_ref, gq_ref, gkv_ref, wq_ref, wkv_ref,
                 auv_ref, bqh_ref, bql_ref, bk_ref, bvt_ref, km_ref, c3_ref, dqt_ref, dk_ref, dvt_ref):
    tm = SUB_IN
    rows = slice(blk * tm, (blk + 1) * tm)
    h = _rms(x_ref[rows, :], g_ref[...]).astype(BF16)

    def proj(a, b):
        return jnp.dot(h, w1_ref[:, a:b], preferred_element_type=F32)

    d = proj(2048, 2560)
    cq = d[:, 0:256]
    lane = lax.broadcasted_iota(jnp.int32, cq.shape, 1)
    ms = jnp.sum(jnp.where(lane < D_Q_LORA, cq * cq, 0.0), axis=-1, keepdims=True) * (1.0 / D_Q_LORA)
    cqn = (cq * lax.rsqrt(ms + NORM_EPS) * gq_ref[...]).astype(BF16)
    qq = jnp.dot(cqn, wq_ref[...], preferred_element_type=F32)
    cq_t = jnp.concatenate([tab_ref[rows, 256:384]] * N_HEADS, axis=1)
    sq_t = jnp.concatenate([tab_ref[rows, 384:512]] * N_HEADS, axis=1)
    qd = (qq[:, 0:512] * cq_t + qq[:, 512:1024] * sq_t) * MLA_QSCALE
    qdt = qd.T.astype(BF16)

    kvn = _rms(d[:, 384:512], gkv_ref[...]).astype(BF16)
    kv = jnp.dot(kvn, wkv_ref[...], preferred_element_type=F32)
    kr = d[:, 128:256] * tab_ref[rows, 512:640] + d[:, 256:384] * tab_ref[rows, 384:512]
    dk_ref[rows, :] = (kv[:, 0:512] + jnp.concatenate([kr] * N_HEADS, axis=1)).astype(BF16)
    vdt = kv[:, 512:768].T.astype(BF16)
    for j in range(tm // TQ):
        dqt_ref[blk * (tm // TQ) + j] = qdt[:, j * TQ:(j + 1) * TQ]
    for j in range(tm // TK):
        dvt_ref[blk * (tm // TK) + j] = vdt[:, j * TK:(j + 1) * TK]

    cb = tab_ref[rows, 0:128]
    sb = tab_ref[rows, 128:256]

    def rope(y):
        y0, y1 = y[:, 0:LANE], y[:, LANE:2 * LANE]
        return jnp.concatenate([y0 * cb - y1 * sb, y1 * cb + y0 * sb], axis=1)

    q = rope(proj(512, 768)) * MOBA_QSCALE
    k = rope(proj(768, 1024))
    v = proj(1024, 1280)
    qt = q.T
    qt_hi = qt.astype(BF16)
    qt_lo = (qt - qt_hi.astype(F32)).astype(BF16)
    vt = v.T.astype(BF16)
    bk_ref[rows, :] = k.astype(BF16)
    for j in range(tm // TQ):
        bqh_ref[blk * (tm // TQ) + j] = qt_hi[:, j * TQ:(j + 1) * TQ]
        bql_ref[blk * (tm // TQ) + j] = qt_lo[:, j * TQ:(j + 1) * TQ]
    for j in range(tm // TK):
        bvt_ref[blk * (tm // TK) + j] = vt[:, j * TK:(j + 1) * TK]
    for j in range(tm // MOBA_BLOCK):
        kb = k[j * MOBA_BLOCK:(j + 1) * MOBA_BLOCK, :]
        jj = blk * (tm // MOBA_BLOCK) + j
        km_ref[0, jj:jj + 1, :] = jnp.sum(kb, axis=0, keepdims=True) * (1.0 / MOBA_BLOCK)

    auv_ref[rows, :] = proj(0, 512)
    c3_ref[rows, :] = proj(1280, 2048)


def _inproj_kernel(x_ref, *refs):
    for blk in range(x_ref.shape[0] // SUB_IN):
        _inproj_rows(blk, x_ref, *refs)


def _inproj(x2, l, p, tab, seq):
    t = x2.shape[0]
    tm = TM_IN
    nt = t // tm
    spt = seq // tm
    rows = lambda w: pl.BlockSpec((tm, w), lambda i: (i, 0))
    tblk = lambda r: pl.BlockSpec((tm // TQ, r, TQ), lambda i: (i, 0, 0))
    vblk = pl.BlockSpec((tm // TK, MIX, TK), lambda i: (i, 0, 0))
    out_shape = (
        jax.ShapeDtypeStruct((t, 512), F32),
        jax.ShapeDtypeStruct((t // TQ, MIX, TQ), BF16),
        jax.ShapeDtypeStruct((t // TQ, MIX, TQ), BF16),
        jax.ShapeDtypeStruct((t, MIX), BF16),
        jax.ShapeDtypeStruct((t // TK, MIX, TK), BF16),
        jax.ShapeDtypeStruct((nt, tm // MOBA_BLOCK, MIX), F32),
        jax.ShapeDtypeStruct((t, 768), F32),
        jax.ShapeDtypeStruct((t // TQ, 512, TQ), BF16),
        jax.ShapeDtypeStruct((t, 512), BF16),
        jax.ShapeDtypeStruct((t // TK, MIX, TK), BF16),
    )
    out_specs = (
        rows(512), tblk(MIX), tblk(MIX), rows(MIX), vblk,
        pl.BlockSpec((1, tm // MOBA_BLOCK, MIX), lambda i: (i, 0, 0)),
        rows(768), tblk(512), rows(512), vblk,
    )
    params = (p["g_pre_mix"], p["w1"], p["gq"], p["gkv"], p["wq"], p["wkv"])
    in_specs = [rows(D_MODEL), pl.BlockSpec((tm, 640), lambda i: (i % spt, 0))]
    in_specs += [_layer_spec(a, l) for a in params]
    return pl.pallas_call(
        _inproj_kernel, grid=(nt,), in_specs=in_specs, out_specs=out_specs, out_shape=out_shape,
        compiler_params=pltpu.CompilerParams(dimension_semantics=("arbitrary",),
                                             vmem_limit_bytes=VMEM_LIMIT),
        name="inproj",
    )(x2, tab, *params)


def _attend(make_scores, vt_ref, s_ref, o_ref, k_windows):
    hs = [slice(h * HEAD_DIM, (h + 1) * HEAD_DIM) for h in range(N_HEADS)]
    row = lax.broadcasted_iota(jnp.int32, (MOBA_BLOCK, TQ), 0)
    col = lax.broadcasted_iota(jnp.int32, (MOBA_BLOCK, TQ), 1)
    causal = row <= col
    ones = jnp.ones((L_ROWS, TK), BF16)
    units = [(sub, h) for sub in range(2) for h in range(N_HEADS)]
    n_slots = s_ref.shape[0]

    def body(k):
        scorers = [make_scores(sub) for sub in range(2)]
        m, acc, out = {}, {}, {}

        def keys_in(sub, w):
            return TK if w < k else (sub + 1) * MOBA_BLOCK

        def score_step(t, u, w):
            sub, h = units[u]
            n = keys_in(sub, w)
            s, bias = scorers[sub](w, h, n)
            if w == k:
                own = jnp.where(causal, s[n - MOBA_BLOCK:n], NEG_INF)
                s = own if n == MOBA_BLOCK else jnp.concatenate([s[0:n - MOBA_BLOCK], own], axis=0)
            s_ref[t % n_slots, 0:n] = s
            mw = None
            for j in range(n // MOBA_BLOCK):
                mb = jnp.max(s[j * MOBA_BLOCK:(j + 1) * MOBA_BLOCK], axis=0, keepdims=True)
                mb = mb if bias is None else mb + bias[j]
                mw = mb if mw is None else jnp.maximum(mw, mb)
            return mw, bias

        def value_step(t, u, w, mw_bias):
            sub, h = units[u]
            n = keys_in(sub, w)
            mw, bias = mw_bias
            m_new = mw if u not in m else jnp.maximum(m[u], mw)
            if bias is None:
                p = jnp.exp2(s_ref[t % n_slots, 0:n] - m_new)
            else:
                p = jnp.concatenate(
                    [jnp.exp2(s_ref[t % n_slots, j * MOBA_BLOCK:(j + 1) * MOBA_BLOCK] - (m_new - bias[j]))
                     for j in range(n // MOBA_BLOCK)], axis=0)
            p = p.astype(BF16)
            v_aug = jnp.concatenate([vt_ref[w, hs[h], 0:n], ones[:, 0:n]], axis=0)
            r = jnp.dot(v_aug, p, preferred_element_type=F32)
            acc[u] = r if u not in acc else jnp.exp2(m[u] - m_new) * acc[u] + r
            m[u] = m_new
            if w == last_window:
                out[u] = acc[u][0:HEAD_DIM, :] * (1.0 / acc[u][HEAD_DIM:HEAD_DIM + 1, :])

        windows = [k] + list(range(k))
        last_window = windows[-1]
        steps = [(u, w) for u in range(len(units)) for w in windows]
        queue = []
        for t, (u, w) in enumerate(steps):
            queue.append((t, u, w, score_step(t, u, w)))
            if len(queue) > PIPE_DEPTH:
                value_step(*queue.pop(0))
        while queue:
            value_step(*queue.pop(0))
        for sub in range(2):
            tile = jnp.concatenate([out[sub * N_HEADS + h] for h in range(N_HEADS)], axis=0).T
            o_ref[sub * TQ:(sub + 1) * TQ, :] = tile.astype(o_ref.dtype)

    for k in range(vt_ref.shape[0]):
        pl.when(k_windows == k)(functools.partial(body, k))


def _moba_kernel(qh_ref, ql_ref, k_ref, vt_ref, km_ref, o_ref, qm_s, s_ref):
    i2 = pl.program_id(1)
    row = lax.broadcasted_iota(jnp.int32, (MIX, TQ), 0)
    for sub in range(2):
        qt_hi = qh_ref[sub]
        for h in range(N_HEADS):
            qm_s[sub, h] = jnp.where(_moba_head_mask(row, h), qt_hi, jnp.zeros_like(qt_hi))

    def make_scores(sub):
        i = 2 * i2 + sub
        km = km_ref[0]
        nb = km.shape[0]
        lane = lax.broadcasted_iota(jnp.int32, km.shape, 1)
        kmh = jnp.concatenate([jnp.where(_moba_head_mask(lane, h), km, 0.0) for h in range(N_HEADS)], axis=0)
        km_hi = kmh.astype(BF16)
        km_lo = (kmh - km_hi.astype(F32)).astype(BF16)
        gate = (jnp.dot(km_hi, qh_ref[sub], preferred_element_type=F32)
                + jnp.dot(km_hi, ql_ref[sub], preferred_element_type=F32)
                + jnp.dot(km_lo, qh_ref[sub], preferred_element_type=F32))

        blk = lax.broadcasted_iota(jnp.int32, (nb, TQ), 0)
        blkf = blk.astype(F32)
        biases = []
        for h in range(N_HEADS):
            g = jnp.where(blk < i, gate[h * nb:(h + 1) * nb, :], NEG_INF)
            bias = jnp.full((nb, TQ), NEG_INF, F32)
            for _ in range(MOBA_TOPK):
                mx = jnp.max(g, axis=0, keepdims=True)
                first = jnp.min(jnp.where(g == mx, blkf, float(nb)), axis=0, keepdims=True)
                hit = blkf == first
                bias = jnp.where(hit & (mx > NEG_INF), 0.0, bias)
                g = jnp.where(hit, NEG_INF, g)
            biases.append(jnp.where(blk == i, 0.0, bias))

        def scores(w, h, n):
            s = jnp.dot(k_ref[w * TK:w * TK + n, :], qm_s[sub, h], preferred_element_type=F32)
            b = biases[h]
            return s, [b[2 * w + j:2 * w + j + 1, :] for j in range(n // MOBA_BLOCK)]

        return scores

    _attend(make_scores, vt_ref, s_ref, o_ref, i2)


def _moba(qh, ql, k, vt, km, batch, seq):
    nq2 = seq // (2 * TQ)
    t = batch * seq
    qspec = pl.BlockSpec((2, MIX, TQ), lambda b, i: (b * nq2 + i, 0, 0))
    return pl.pallas_call(
        _moba_kernel, grid=(batch, nq2),
        in_specs=[qspec, qspec,
                  pl.BlockSpec((seq, MIX), lambda b, i: (b, 0)),
                  pl.BlockSpec((seq // TK, MIX, TK), lambda b, i: (b, 0, 0)),
                  pl.BlockSpec((1, seq // MOBA_BLOCK, MIX), lambda b, i: (b, 0, 0))],
        out_specs=pl.BlockSpec((2 * TQ, MIX), lambda b, i: (b * nq2 + i, 0)),
        out_shape=jax.ShapeDtypeStruct((t, MIX), BF16),
        scratch_shapes=[pltpu.VMEM((2, N_HEADS, MIX, TQ), BF16),
                        pltpu.VMEM((S_SLOTS, TK, TQ), F32)],
        compiler_params=pltpu.CompilerParams(dimension_semantics=("arbitrary", "arbitrary"),
                                             vmem_limit_bytes=VMEM_LIMIT),
        name="moba",
    )(qh, ql, k, vt, km)


def _mla_kernel(qt_ref, k_ref, vt_ref, o_ref, s_ref):
    def make_scores(sub):
        def scores(w, h, n):
            kw = k_ref[w * TK:w * TK + n, h * LANE:(h + 1) * LANE]
            return jnp.dot(kw, qt_ref[sub, h * LANE:(h + 1) * LANE, :], preferred_element_type=F32), None
        return scores

    _attend(make_scores, vt_ref, s_ref, o_ref, pl.program_id(1))


def _mla(qt, k, vt, batch, seq):
    nq2 = seq // (2 * TQ)
    t = batch * seq
    return pl.pallas_call(
        _mla_kernel, grid=(batch, nq2),
        in_specs=[pl.BlockSpec((2, 512, TQ), lambda b, i: (b * nq2 + i, 0, 0)),
                  pl.BlockSpec((seq, 512), lambda b, i: (b, 0)),
                  pl.BlockSpec((seq // TK, MIX, TK), lambda b, i: (b, 0, 0))],
        out_specs=pl.BlockSpec((2 * TQ, MIX), lambda b, i: (b * nq2 + i, 0)),
        out_shape=jax.ShapeDtypeStruct((t, MIX), BF16),
        scratch_shapes=[pltpu.VMEM((S_SLOTS, TK, TQ), F32)],
        compiler_params=pltpu.CompilerParams(dimension_semantics=("arbitrary", "arbitrary"),
                                             vmem_limit_bytes=VMEM_LIMIT),
        name="mla",
    )(qt, k, vt)


def _merge_rows(r0, first_in_seq, x_ref, auv_ref, c3_ref, c3p_ref, ob_ref, od_ref,
                g_ref, wg_ref, lng_ref, lnb_ref, ws, bs_ref, wc_ref, wb_ref, wo_ref, gp_ref, o_ref):
    rows = slice(r0, r0 + SUB_MERGE)
    y_b = jnp.dot(ob_ref[rows, :], wb_ref[1], preferred_element_type=F32)
    y_d = jnp.dot(od_ref[rows, :], wb_ref[3], preferred_element_type=F32)
    x = x_ref[rows, :]
    h = _rms(x, g_ref[...]).astype(BF16)

    u = _gelu(auv_ref[rows, 0:MIX])
    v = _gelu(auv_ref[rows, MIX:2 * MIX])
    mu = jnp.mean(v, axis=-1, keepdims=True)
    vc = v - mu
    var = jnp.mean(vc * vc, axis=-1, keepdims=True)
    v = (vc * lax.rsqrt(var + LN_EPS) * lng_ref[...] + lnb_ref[...]).astype(BF16)
    lane = lax.broadcasted_iota(jnp.int32, (A_CHUNK, MIX), 1)
    svs = []
    for cidx in range(SUB_MERGE // A_CHUNK):
        r = jnp.dot(ws, v[cidx * A_CHUNK:(cidx + 1) * A_CHUNK, :], preferred_element_type=F32)
        sv = jnp.where(lane < 64, r[0:128],
                       jnp.where(lane < 128, r[128:256], jnp.where(lane < 192, r[256:384], r[384:512])))
        svs.append(sv + bs_ref[...])
    out_a = (u * jnp.concatenate(svs, axis=0)).astype(BF16)

    z = c3_ref[rows, MIX:2 * MIX] * c3_ref[rows, 2 * MIX:3 * MIX]
    if r0 == 0:
        zp = c3p_ref[:, MIX:2 * MIX] * c3p_ref[:, 2 * MIX:3 * MIX]
        zp = jnp.where(first_in_seq, 0.0, zp)
    else:
        zp = c3_ref[r0 - 8:r0, MIX:2 * MIX] * c3_ref[r0 - 8:r0, 2 * MIX:3 * MIX]
    zm1, zm2 = zp[7:8, :], zp[6:7, :]
    rowi = lax.broadcasted_iota(jnp.int32, z.shape, 0)
    z1 = jnp.where(rowi == 0, zm1, pltpu.roll(z, 1, axis=0))
    z2 = jnp.where(rowi == 0, zm2, jnp.where(rowi == 1, zm1, pltpu.roll(z, 2, axis=0)))
    conv = wc_ref[0:1, :] * z2 + wc_ref[1:2, :] * z1 + wc_ref[2:3, :] * z
    out_c = (c3_ref[rows, 0:MIX] * conv).astype(BF16)

    def gate(bi):
        logits = jnp.dot(h, wg_ref[:, bi * D_MODEL:(bi + 1) * D_MODEL], preferred_element_type=F32)
        return 1.0 / (1.0 + jnp.exp(-logits))

    merged = gate(1) * y_b + gate(3) * y_d
    merged = merged + gate(0) * jnp.dot(out_a, wb_ref[0], preferred_element_type=F32)
    merged = merged + gate(2) * jnp.dot(out_c, wb_ref[2], preferred_element_type=F32)
    y = jnp.dot(merged.astype(BF16), wo_ref[...], preferred_element_type=F32)
    o_ref[rows, :] = x + _rms(y, gp_ref[...])


def _merge_kernel(x_ref, auv_ref, c3_ref, c3p_ref, ob_ref, od_ref,
                  g_ref, wg_ref, lng_ref, lnb_ref, ws_ref, bs_ref, wc_ref, wb_ref, wo_ref, gp_ref,
                  o_ref, *, steps_per_seq):
    wrow = lax.broadcasted_iota(jnp.int32, ws_ref.shape, 0) & (A_CHUNK - 1)
    wcol = lax.broadcasted_iota(jnp.int32, ws_ref.shape, 1)
    ws = jnp.where(wrow >= wcol, ws_ref[...], 0.0).astype(BF16)
    first_in_seq = pl.program_id(0) % steps_per_seq == 0
    for r0 in range(0, x_ref.shape[0], SUB_MERGE):
        _merge_rows(r0, first_in_seq, x_ref, auv_ref, c3_ref, c3p_ref, ob_ref, od_ref,
                    g_ref, wg_ref, lng_ref, lnb_ref, ws, bs_ref, wc_ref, wb_ref, wo_ref, gp_ref, o_ref)


def _merge(x2, auv, c3, ob, od, l, p, seq):
    t = x2.shape[0]
    tm = TM_MERGE
    rows = lambda w: pl.BlockSpec((tm, w), lambda i: (i, 0))
    prev8 = pl.BlockSpec((8, 768), lambda i: (jnp.maximum(i * (tm // 8) - 1, 0), 0))
    params = (p["g_pre_mix"], p["w_gate"], p["a_ln_g"], p["a_ln_b"], p["a_w_s"], p["a_b_s"],
              p["c_w_conv"], p["w_branch"], p["w_out"], p["g_post_mix"])
    in_specs = [rows(D_MODEL), rows(512), rows(768), prev8, rows(MIX), rows(MIX)]
    in_specs += [_layer_spec(a, l) for a in params]
    return pl.pallas_call(
        functools.partial(_merge_kernel, steps_per_seq=seq // tm), grid=(t // tm,),
        in_specs=in_specs, out_specs=rows(D_MODEL),
        out_shape=jax.ShapeDtypeStruct((t, D_MODEL), F32),
        compiler_params=pltpu.CompilerParams(dimension_semantics=("arbitrary",),
                                             vmem_limit_bytes=VMEM_LIMIT),
        name="merge",
    )(x2, auv, c3, c3, ob, od, *params)


def _mlp_kernel(x_ref, g_ref, w1_ref, w2_ref, gp_ref, o_ref):
    d_ff = w1_ref.shape[1]
    for r0 in range(0, x_ref.shape[0], SUB_MLP):
        rows = slice(r0, r0 + SUB_MLP)
        x = x_ref[rows, :]
        h = _rms(x, g_ref[...]).astype(BF16)
        y = None
        for cidx in range(d_ff // D_MODEL):
            sl = slice(cidx * D_MODEL, (cidx + 1) * D_MODEL)
            a = jnp.maximum(jnp.dot(h, w1_ref[:, sl], preferred_element_type=F32), 0.0)
            part = jnp.dot((a * a).astype(BF16), w2_ref[sl, :], preferred_element_type=F32)
            y = part if y is None else y + part
        o_ref[rows, :] = x + _rms(y, gp_ref[...])


def _mlp(x2, l, p):
    t = x2.shape[0]
    tm = TM_MLP
    rows = pl.BlockSpec((tm, D_MODEL), lambda i: (i, 0))
    params = (p["g_pre_mlp"], p["w_mlp_in"], p["w_mlp_out"], p["g_post_mlp"])
    return pl.pallas_call(
        _mlp_kernel, grid=(t // tm,),
        in_specs=[rows] + [_layer_spec(a, l) for a in params],
        out_specs=rows, out_shape=jax.ShapeDtypeStruct((t, D_MODEL), F32),
        compiler_params=pltpu.CompilerParams(dimension_semantics=("arbitrary",),
                                             vmem_limit_bytes=VMEM_LIMIT),
        name="mlp",
    )(x2, *params)


def _rope_tables(dim, seq):
    inv = 1.0 / (ROPE_THETA ** (jnp.arange(0, dim, 2, dtype=F32) / dim))
    ang = jnp.arange(seq, dtype=F32)[:, None] * inv[None, :]
    return jnp.cos(ang), jnp.sin(ang)


def _position_tables(seq):
    cos_b, sin_b = _rope_tables(B_ROT, seq)
    cos_d, sin_d = _rope_tables(D_ROPE, seq)
    one, zero = jnp.ones((seq, 1), F32), jnp.zeros((seq, 1), F32)
    cb = jnp.concatenate([jnp.tile(cos_b, (1, N_HEADS)), jnp.tile(one, (1, 96))], axis=1)
    sb = jnp.concatenate([jnp.tile(sin_b, (1, N_HEADS)), jnp.tile(zero, (1, 96))], axis=1)
    cq = jnp.concatenate([jnp.tile(one, (1, 64)), cos_d, cos_d, jnp.tile(zero, (1, 32))], axis=1)
    sq = jnp.concatenate([jnp.tile(zero, (1, 64)), sin_d, sin_d, jnp.tile(zero, (1, 32))], axis=1)
    ck = jnp.concatenate([jnp.tile(zero, (1, 64)), cos_d, cos_d, jnp.tile(zero, (1, 32))], axis=1)
    return jnp.concatenate([cb, sb, cq, sq, ck], axis=1)


def _pack_params(g_pre_mix, w_in, a_ln_g, a_ln_b, a_w_s, a_b_s, c_w_conv, d_q_norm_g, d_w_uq, d_kv_norm_g,
                 d_w_ukv, w_branch, w_out, g_post_mix, g_pre_mlp, w_mlp_in, w_mlp_out, g_post_mlp):
    depth = w_in.shape[0]
    wb = w_in.astype(BF16)
    z = lambda n: jnp.zeros((depth, D_MODEL, n), BF16)
    a = wb[..., 0:512]
    bq, bk, bv = wb[..., 512:768], wb[..., 768:1024], wb[..., 1024:1280]
    c = wb[..., 1280:2048]
    cq, ckv, kr = wb[..., 2048:2240], wb[..., 2240:2368], wb[..., 2368:2400]
    kr_sw = jnp.concatenate([-kr[..., 16:32], kr[..., 0:16]], axis=-1)
    w1 = jnp.concatenate([a, _moba_cols(bq), _moba_cols(bk), bv, c,
                          cq[..., 0:128], cq[..., 128:192], kr, z(32),
                          z(64), kr_sw, z(32), ckv], axis=-1)

    nope, x1, x2 = d_w_uq[..., 0:64], d_w_uq[..., 64:80], d_w_uq[..., 80:96]
    zq = lambda n: jnp.zeros((depth, D_Q_LORA, N_HEADS, n), d_w_uq.dtype)
    main = jnp.concatenate([nope, x1, x2, zq(32)], axis=-1).reshape(depth, D_Q_LORA, 512)
    swap = jnp.concatenate([zq(64), -x2, x1, zq(32)], axis=-1).reshape(depth, D_Q_LORA, 512)
    wq = jnp.concatenate([main, swap], axis=-1)
    wq = jnp.concatenate([wq, jnp.zeros((depth, 256 - D_Q_LORA, 1024), wq.dtype)], axis=1).astype(BF16)
    gq = jnp.concatenate([d_q_norm_g, jnp.zeros((depth, 256 - D_Q_LORA), F32)], axis=-1)

    wk = jnp.concatenate([d_w_ukv[..., 0:64], jnp.zeros((depth, D_KV_LORA, N_HEADS, 64), d_w_ukv.dtype)],
                         axis=-1).reshape(depth, D_KV_LORA, 512)
    wv = d_w_ukv[..., 64:128].reshape(depth, D_KV_LORA, MIX)
    row = lambda g: g[:, None, :]
    return {
        "g_pre_mix": row(g_pre_mix), "w1": w1, "gq": row(gq), "gkv": row(d_kv_norm_g), "wq": wq,
        "wkv": jnp.concatenate([wk, wv], axis=-1).astype(BF16),
        "w_gate": wb[..., 2400:],
        "a_ln_g": row(a_ln_g), "a_ln_b": row(a_ln_b),
        "a_w_s": a_w_s.reshape(depth, N_HEADS * A_CHUNK, A_CHUNK),
        "a_b_s": jnp.repeat(jnp.swapaxes(a_b_s, 1, 2), HEAD_DIM, axis=2),
        "c_w_conv": c_w_conv, "w_branch": w_branch.astype(BF16), "w_out": w_out.astype(BF16),
        "g_post_mix": row(g_post_mix), "g_pre_mlp": row(g_pre_mlp),
        "w_mlp_in": w_mlp_in.astype(BF16), "w_mlp_out": w_mlp_out.astype(BF16),
        "g_post_mlp": row(g_post_mlp),
    }


def kernel(x, g_pre_mix, w_in, a_ln_g, a_ln_b, a_w_s, a_b_s, c_w_conv, d_q_norm_g, d_w_uq,
           d_kv_norm_g, d_w_ukv, w_branch, w_out, g_post_mix, g_pre_mlp, w_mlp_in, w_mlp_out,
           g_post_mlp):
    batch, seq, _ = x.shape
    t = batch * seq
    tab = _position_tables(seq)
    p = _pack_params(g_pre_mix, w_in, a_ln_g, a_ln_b, a_w_s, a_b_s, c_w_conv, d_q_norm_g, d_w_uq,
                     d_kv_norm_g, d_w_ukv, w_branch, w_out, g_post_mix, g_pre_mlp, w_mlp_in,
                     w_mlp_out, g_post_mlp)
    x2 = x.reshape(t, D_MODEL)
    for l in range(w_in.shape[0]):
        (auv, bqh, bql, bk, bvt, km, c3, dqt, dk, dvt) = _inproj(x2, l, p, tab, seq)
        km = km.reshape(batch, seq // MOBA_BLOCK, MIX)
        out_b = _moba(bqh, bql, bk, bvt, km, batch, seq)
        out_d = _mla(dqt, dk, dvt, batch, seq)
        x2 = _merge(x2, auv, c3, out_b, out_d, l, p, seq)
        x2 = _mlp(x2, l, p)
    return x2.reshape(batch, seq, D_MODEL)
```

```python
import functools
import math

import jax
import jax.numpy as jnp
from jax import lax
from jax.experimental import pallas as pl
from jax.experimental.pallas import tpu as pltpu

F32 = jnp.float32
BF16 = jnp.bfloat16

D_MODEL = 1024
HEAD_DIM = 64
N_HEADS = 4
MIX = 256
A_CHUNK = 128
MOBA_BLOCK = 256
MOBA_TOPK = 3
B_ROT = 16
D_NOPE, D_ROPE, D_Q_LORA, D_KV_LORA = 64, 32, 192, 128
ROPE_THETA = 500000.0
NORM_EPS = 1e-6
LN_EPS = 1e-5
LOG2E = 1.4426950408889634
MOBA_QSCALE = (HEAD_DIM ** -0.5) * LOG2E
MLA_QSCALE = ((D_NOPE + D_ROPE) ** -0.5) * LOG2E

LANE = 128
TM_IN = 1024
SUB_IN = 512
TM_MERGE = 1024
SUB_MERGE = 256
TM_MLP = 1024
SUB_MLP = 512
TQ = 256
TK = 512
L_ROWS = 16
PIPE_DEPTH = 4
S_SLOTS = PIPE_DEPTH + 2
VMEM_LIMIT = 56 * 1024 * 1024

N1 = 2560
NEG_INF = float("-inf")


def _moba_cols(w):
    lead = w.shape[:-1]
    w4 = w.reshape(lead + (N_HEADS, HEAD_DIM))
    part = lambda a, b: w4[..., a:b].reshape(lead + (N_HEADS * (b - a),))
    return jnp.concatenate([part(0, 8), part(16, 40), part(8, 16), part(40, 64)], axis=-1)


def _moba_head_mask(idx, h):
    l = idx & (LANE - 1)
    return ((l >= 8 * h) & (l < 8 * h + 8)) | ((l >= 32 + 24 * h) & (l < 56 + 24 * h))


def _rms(x, g):
    return x * lax.rsqrt(jnp.mean(x * x, axis=-1, keepdims=True) + NORM_EPS) * g


def _gelu(x):
    return 0.5 * x * (1.0 + jnp.tanh(math.sqrt(2.0 / math.pi) * (x + 0.044715 * (x * x * x))))


def _layer_spec(arr, l):
    nd = arr.ndim - 1
    return pl.BlockSpec((None,) + arr.shape[1:], lambda *_: (l,) + (0,) * nd, pipeline_mode=pl.Buffered(1))


def _inproj_rows(blk, x_ref, tab_ref, g_ref, w1_ref, gq_ref, gkv_ref, wq_ref, wkv_ref,
                 auv_ref, bqh_ref, bql_ref, bk_ref, bvt_ref, km_ref, c3_ref, dqt_ref, dk_ref, dvt_ref):
    tm = SUB_IN
    rows = slice(blk * tm, (blk + 1) * tm)
    h = _rms(x_ref[rows, :], g_ref[...]).astype(BF16)

    def proj(a, b):
        return jnp.dot(h, w1_ref[:, a:b], preferred_element_type=F32)

    d = proj(2048, 2560)
    cq = d[:, 0:256]
    lane = lax.broadcasted_iota(jnp.int32, cq.shape, 1)
    ms = jnp.sum(jnp.where(lane < D_Q_LORA, cq * cq, 0.0), axis=-1, keepdims=True) * (1.0 / D_Q_LORA)
    cqn = (cq * lax.rsqrt(ms + NORM_EPS) * gq_ref[...]).astype(BF16)
    qq = jnp.dot(cqn, wq_ref[...], preferred_element_type=F32)
    cq_t = jnp.concatenate([tab_ref[rows, 256:384]] * N_HEADS, axis=1)
    sq_t = jnp.concatenate([tab_ref[rows, 384:512]] * N_HEADS, axis=1)
    qd = (qq[:, 0:512] * cq_t + qq[:, 512:1024] * sq_t) * MLA_QSCALE
    qdt = qd.T.astype(BF16)

    kvn = _rms(d[:, 384:512], gkv_ref[...]).astype(BF16)
    kv = jnp.dot(kvn, wkv_ref[...], preferred_element_type=F32)
    kr = d[:, 128:256] * tab_ref[rows, 512:640] + d[:, 256:384] * tab_ref[rows, 384:512]
    dk_ref[rows, :] = (kv[:, 0:512] + jnp.concatenate([kr] * N_HEADS, axis=1)).astype(BF16)
    vdt = kv[:, 512:768].T.astype(BF16)
    for j in range(tm // TQ):
        dqt_ref[blk * (tm // TQ) + j] = qdt[:, j * TQ:(j + 1) * TQ]
    for j in range(tm // TK):
        dvt_ref[blk * (tm // TK) + j] = vdt[:, j * TK:(j + 1) * TK]

    cb = tab_ref[rows, 0:128]
    sb = tab_ref[rows, 128:256]

    def rope(y):
        y0, y1 = y[:, 0:LANE], y[:, LANE:2 * LANE]
        return jnp.concatenate([y0 * cb - y1 * sb, y1 * cb + y0 * sb], axis=1)

    q = rope(proj(512, 768)) * MOBA_QSCALE
    k = rope(proj(768, 1024))
    v = proj(1024, 1280)
    qt = q.T
    qt_hi = qt.astype(BF16)
    qt_lo = (qt - qt_hi.astype(F32)).astype(BF16)
    vt = v.T.astype(BF16)
    bk_ref[rows, :] = k.astype(BF16)
    for j in range(tm // TQ):
        bqh_ref[blk * (tm // TQ) + j] = qt_hi[:, j * TQ:(j + 1) * TQ]
        bql_ref[blk * (tm // TQ) + j] = qt_lo[:, j * TQ:(j + 1) * TQ]
    for j in range(tm // TK):
        bvt_ref[blk * (tm // TK) + j] = vt[:, j * TK:(j + 1) * TK]
    for j in range(tm // MOBA_BLOCK):
        kb = k[j * MOBA_BLOCK:(j + 1) * MOBA_BLOCK, :]
        jj = blk * (tm // MOBA_BLOCK) + j
        km_ref[0, jj:jj + 1, :] = jnp.sum(kb, axis=0, keepdims=True) * (1.0 / MOBA_BLOCK)

    auv_ref[rows, :] = proj(0, 512)
    c3_ref[rows, :] = proj(1280, 2048)


def _inproj_kernel(x_ref, *refs):
    for blk in range(x_ref.shape[0] // SUB_IN):
        _inproj_rows(blk, x_ref, *refs)


def _inproj(x2, l, p, tab, seq):
    t = x2.shape[0]
    tm = TM_IN
    nt = t // tm
    spt = seq // tm
    rows = lambda w: pl.BlockSpec((tm, w), lambda i: (i, 0))
    tblk = lambda r: pl.BlockSpec((tm // TQ, r, TQ), lambda i: (i, 0, 0))
    vblk = pl.BlockSpec((tm // TK, MIX, TK), lambda i: (i, 0, 0))
    out_shape = (
        jax.ShapeDtypeStruct((t, 512), F32),
        jax.ShapeDtypeStruct((t // TQ, MIX, TQ), BF16),
        jax.ShapeDtypeStruct((t // TQ, MIX, TQ), BF16),
        jax.ShapeDtypeStruct((t, MIX), BF16),
        jax.ShapeDtypeStruct((t // TK, MIX, TK), BF16),
        jax.ShapeDtypeStruct((nt, tm // MOBA_BLOCK, MIX), F32),
        jax.ShapeDtypeStruct((t, 768), F32),
        jax.ShapeDtypeStruct((t // TQ, 512, TQ), BF16),
        jax.ShapeDtypeStruct((t, 512), BF16),
        jax.ShapeDtypeStruct((t // TK, MIX, TK), BF16),
    )
    out_specs = (
        rows(512), tblk(MIX), tblk(MIX), rows(MIX), vblk,
        pl.BlockSpec((1, tm // MOBA_BLOCK, MIX), lambda i: (i, 0, 0)),
        rows(768), tblk(512), rows(512), vblk,
    )
    params = (p["g_pre_mix"], p["w1"], p["gq"], p["gkv"], p["wq"], p["wkv"])
    in_specs = [rows(D_MODEL), pl.BlockSpec((tm, 640), lambda i: (i % spt, 0))]
    in_specs += [_layer_spec(a, l) for a in params]
    return pl.pallas_call(
        _inproj_kernel, grid=(nt,), in_specs=in_specs, out_specs=out_specs, out_shape=out_shape,
        compiler_params=pltpu.CompilerParams(dimension_semantics=("arbitrary",),
                                             vmem_limit_bytes=VMEM_LIMIT),
        name="inproj",
    )(x2, tab, *params)


def _attend(make_scores, vt_ref, s_ref, o_ref, k_windows):
    hs = [slice(h * HEAD_DIM, (h + 1) * HEAD_DIM) for h in range(N_HEADS)]
    row = lax.broadcasted_iota(jnp.int32, (MOBA_BLOCK, TQ), 0)
    col = lax.broadcasted_iota(jnp.int32, (MOBA_BLOCK, TQ), 1)
    causal = row <= col
    ones = jnp.ones((L_ROWS, TK), BF16)
    units = [(sub, h) for sub in range(2) for h in range(N_HEADS)]
    n_slots = s_ref.shape[0]

    def body(k):
        scorers = [make_scores(sub) for sub in range(2)]
        m, acc, out = {}, {}, {}

        def keys_in(sub, w):
            return TK if w < k else (sub + 1) * MOBA_BLOCK

        def score_step(t, u, w):
            sub, h = units[u]
            n = keys_in(sub, w)
            s, bias = scorers[sub](w, h, n)
            if w == k:
                own = jnp.where(causal, s[n - MOBA_BLOCK:n], NEG_INF)
                s = own if n == MOBA_BLOCK else jnp.concatenate([s[0:n - MOBA_BLOCK], own], axis=0)
            s_ref[t % n_slots, 0:n] = s
            mw = None
            for j in range(n // MOBA_BLOCK):
                mb = jnp.max(s[j * MOBA_BLOCK:(j + 1) * MOBA_BLOCK], axis=0, keepdims=True)
                mb = mb if bias is None else mb + bias[j]
                mw = mb if mw is None else jnp.maximum(mw, mb)
            return mw, bias

        def value_step(t, u, w, mw_bias):
            sub, h = units[u]
            n = keys_in(sub, w)
            mw, bias = mw_bias
            m_new = mw if u not in m else jnp.maximum(m[u], mw)
            if bias is None:
                p = jnp.exp2(s_ref[t % n_slots, 0:n] - m_new)
            else:
                p = jnp.concatenate(
                    [jnp.exp2(s_ref[t % n_slots, j * MOBA_BLOCK:(j + 1) * MOBA_BLOCK] - (m_new - bias[j]))
                     for j in range(n // MOBA_BLOCK)], axis=0)
            p = p.astype(BF16)
            v_aug = jnp.concatenate([vt_ref[w, hs[h], 0:n], ones[:, 0:n]], axis=0)
            r = jnp.dot(v_aug, p, preferred_element_type=F32)
            acc[u] = r if u not in acc else jnp.exp2(m[u] - m_new) * acc[u] + r
            m[u] = m_new
            if w == last_window:
                out[u] = acc[u][0:HEAD_DIM, :] * (1.0 / acc[u][HEAD_DIM:HEAD_DIM + 1, :])

        windows = [k] + list(range(k))
        last_window = windows[-1]
        steps = [(u, w) for u in range(len(units)) for w in windows]
        queue = []
        for t, (u, w) in enumerate(steps):
            queue.append((t, u, w, score_step(t, u, w)))
            if len(queue) > PIPE_DEPTH:
                value_step(*queue.pop(0))
        while queue:
            value_step(*queue.pop(0))
        for sub in range(2):
            tile = jnp.concatenate([out[sub * N_HEADS + h] for h in range(N_HEADS)], axis=0).T
            o_ref[sub * TQ:(sub + 1) * TQ, :] = tile.astype(o_ref.dtype)

    for k in range(vt_ref.shape[0]):
        pl.when(k_windows == k)(functools.partial(body, k))


def _moba_kernel(qh_ref, ql_ref, k_ref, vt_ref, km_ref, o_ref, qm_s, s_ref):
    i2 = pl.program_id(1)
    row = lax.broadcasted_iota(jnp.int32, (MIX, TQ), 0)

    def make_scores(sub):
        i = 2 * i2 + sub
        qt_hi = qh_ref[sub]
        for h in range(N_HEADS):
            qm_s[sub, h] = jnp.where(_moba_head_mask(row, h), qt_hi, jnp.zeros_like(qt_hi))
        km = km_ref[0]
        nb = km.shape[0]
        lane = lax.broadcasted_iota(jnp.int32, km.shape, 1)
        kmh = jnp.concatenate([jnp.where(_moba_head_mask(lane, h), km, 0.0) for h in range(N_HEADS)], axis=0)
        km_hi = kmh.astype(BF16)
        km_lo = (kmh - km_hi.astype(F32)).astype(BF16)
        gate = (jnp.dot(km_hi, qh_ref[sub], preferred_element_type=F32)
                + jnp.dot(km_hi, ql_ref[sub], preferred_element_type=F32)
                + jnp.dot(km_lo, qh_ref[sub], preferred_element_type=F32))

        blk = lax.broadcasted_iota(jnp.int32, (nb, TQ), 0)
        blkf = blk.astype(F32)
        biases = []
        for h in range(N_HEADS):
            g = jnp.where(blk < i, gate[h * nb:(h + 1) * nb, :], NEG_INF)
            bias = jnp.full((nb, TQ), NEG_INF, F32)
            for _ in range(MOBA_TOPK):
                mx = jnp.max(g, axis=0, keepdims=True)
                first = jnp.min(jnp.where(g == mx, blkf, float(nb)), axis=0, keepdims=True)
                hit = blkf == first
                bias = jnp.where(hit & (mx > NEG_INF), 0.0, bias)
                g = jnp.where(hit, NEG_INF, g)
            biases.append(jnp.where(blk == i, 0.0, bias))

        def scores(w, h, n):
            s = jnp.dot(k_ref[w * TK:w * TK + n, :], qm_s[sub, h], preferred_element_type=F32)
            b = biases[h]
            return s, [b[2 * w + j:2 * w + j + 1, :] for j in range(n // MOBA_BLOCK)]

        return scores

    _attend(make_scores, vt_ref, s_ref, o_ref, i2)


def _moba(qh, ql, k, vt, km, batch, seq):
    nq2 = seq // (2 * TQ)
    t = batch * seq
    qspec = pl.BlockSpec((2, MIX, TQ), lambda b, i: (b * nq2 + i, 0, 0))
    return pl.pallas_call(
        _moba_kernel, grid=(batch, nq2),
        in_specs=[qspec, qspec,
                  pl.BlockSpec((seq, MIX), lambda b, i: (b, 0)),
                  pl.BlockSpec((seq // TK, MIX, TK), lambda b, i: (b, 0, 0)),
                  pl.BlockSpec((1, seq // MOBA_BLOCK, MIX), lambda b, i: (b, 0, 0))],
        out_specs=pl.BlockSpec((2 * TQ, MIX), lambda b, i: (b * nq2 + i, 0)),
        out_shape=jax.ShapeDtypeStruct((t, MIX), BF16),
        scratch_shapes=[pltpu.VMEM((2, N_HEADS, MIX, TQ), BF16),
                        pltpu.VMEM((S_SLOTS, TK, TQ), F32)],
        compiler_params=pltpu.CompilerParams(dimension_semantics=("arbitrary", "arbitrary"),
                                             vmem_limit_bytes=VMEM_LIMIT),
        name="moba",
    )(qh, ql, k, vt, km)


def _mla_kernel(qt_ref, k_ref, vt_ref, o_ref, s_ref):
    def make_scores(sub):
        def scores(w, h, n):
            kw = k_ref[w * TK:w * TK + n, h * LANE:(h + 1) * LANE]
            return jnp.dot(kw, qt_ref[sub, h * LANE:(h + 1) * LANE, :], preferred_element_type=F32), None
        return scores

    _attend(make_scores, vt_ref, s_ref, o_ref, pl.program_id(1))


def _mla(qt, k, vt, batch, seq):
    nq2 = seq // (2 * TQ)
    t = batch * seq
    return pl.pallas_call(
        _mla_kernel, grid=(batch, nq2),
        in_specs=[pl.BlockSpec((2, 512, TQ), lambda b, i: (b * nq2 + i, 0, 0)),
                  pl.BlockSpec((seq, 512), lambda b, i: (b, 0)),
                  pl.BlockSpec((seq // TK, MIX, TK), lambda b, i: (b, 0, 0))],
        out_specs=pl.BlockSpec((2 * TQ, MIX), lambda b, i: (b * nq2 + i, 0)),
        out_shape=jax.ShapeDtypeStruct((t, MIX), BF16),
        scratch_shapes=[pltpu.VMEM((S_SLOTS, TK, TQ), F32)],
        compiler_params=pltpu.CompilerParams(dimension_semantics=("arbitrary", "arbitrary"),
                                             vmem_limit_bytes=VMEM_LIMIT),
        name="mla",
    )(qt, k, vt)


def _merge_rows(r0, first_in_seq, x_ref, auv_ref, c3_ref, c3p_ref, ob_ref, od_ref,
                g_ref, wg_ref, lng_ref, lnb_ref, ws, bs_ref, wc_ref, wb_ref, wo_ref, gp_ref, o_ref):
    rows = slice(r0, r0 + SUB_MERGE)
    y_b = jnp.dot(ob_ref[rows, :], wb_ref[1], preferred_element_type=F32)
    y_d = jnp.dot(od_ref[rows, :], wb_ref[3], preferred_element_type=F32)
    x = x_ref[rows, :]
    h = _rms(x, g_ref[...]).astype(BF16)

    u = _gelu(auv_ref[rows, 0:MIX])
    v = _gelu(auv_ref[rows, MIX:2 * MIX])
    mu = jnp.mean(v, axis=-1, keepdims=True)
    vc = v - mu
    var = jnp.mean(vc * vc, axis=-1, keepdims=True)
    v = (vc * lax.rsqrt(var + LN_EPS) * lng_ref[...] + lnb_ref[...]).astype(BF16)
    lane = lax.broadcasted_iota(jnp.int32, (A_CHUNK, MIX), 1)
    svs = []
    for cidx in range(SUB_MERGE // A_CHUNK):
        r = jnp.dot(ws, v[cidx * A_CHUNK:(cidx + 1) * A_CHUNK, :], preferred_element_type=F32)
        sv = jnp.where(lane < 64, r[0:128],
                       jnp.where(lane < 128, r[128:256], jnp.where(lane < 192, r[256:384], r[384:512])))
        svs.append(sv + bs_ref[...])
    out_a = (u * jnp.concatenate(svs, axis=0)).astype(BF16)

    z = c3_ref[rows, MIX:2 * MIX] * c3_ref[rows, 2 * MIX:3 * MIX]
    if r0 == 0:
        zp = c3p_ref[:, MIX:2 * MIX] * c3p_ref[:, 2 * MIX:3 * MIX]
        zp = jnp.where(first_in_seq, 0.0, zp)
    else:
        zp = c3_ref[r0 - 8:r0, MIX:2 * MIX] * c3_ref[r0 - 8:r0, 2 * MIX:3 * MIX]
    zm1, zm2 = zp[7:8, :], zp[6:7, :]
    rowi = lax.broadcasted_iota(jnp.int32, z.shape, 0)
    z1 = jnp.where(rowi == 0, zm1, pltpu.roll(z, 1, axis=0))
    z2 = jnp.where(rowi == 0, zm2, jnp.where(rowi == 1, zm1, pltpu.roll(z, 2, axis=0)))
    conv = wc_ref[0:1, :] * z2 + wc_ref[1:2, :] * z1 + wc_ref[2:3, :] * z
    out_c = (c3_ref[rows, 0:MIX] * conv).astype(BF16)

    def gate(bi):
        logits = jnp.dot(h, wg_ref[:, bi * D_MODEL:(bi + 1) * D_MODEL], preferred_element_type=F32)
        return 1.0 / (1.0 + jnp.exp(-logits))

    merged = gate(1) * y_b + gate(3) * y_d
    merged = merged + gate(0) * jnp.dot(out_a, wb_ref[0], preferred_element_type=F32)
    merged = merged + gate(2) * jnp.dot(out_c, wb_ref[2], preferred_element_type=F32)
    y = jnp.dot(merged.astype(BF16), wo_ref[...], preferred_element_type=F32)
    o_ref[rows, :] = x + _rms(y, gp_ref[...])


def _merge_kernel(x_ref, auv_ref, c3_ref, c3p_ref, ob_ref, od_ref,
                  g_ref, wg_ref, lng_ref, lnb_ref, ws_ref, bs_ref, wc_ref, wb_ref, wo_ref, gp_ref,
                  o_ref, *, steps_per_seq):
    wrow = lax.broadcasted_iota(jnp.int32, ws_ref.shape, 0) & (A_CHUNK - 1)
    wcol = lax.broadcasted_iota(jnp.int32, ws_ref.shape, 1)
    ws = jnp.where(wrow >= wcol, ws_ref[...], 0.0).astype(BF16)
    first_in_seq = pl.program_id(0) % steps_per_seq == 0
    for r0 in range(0, x_ref.shape[0], SUB_MERGE):
        _merge_rows(r0, first_in_seq, x_ref, auv_ref, c3_ref, c3p_ref, ob_ref, od_ref,
                    g_ref, wg_ref, lng_ref, lnb_ref, ws, bs_ref, wc_ref, wb_ref, wo_ref, gp_ref, o_ref)


def _merge(x2, auv, c3, ob, od, l, p, seq):
    t = x2.shape[0]
    tm = TM_MERGE
    rows = lambda w: pl.BlockSpec((tm, w), lambda i: (i, 0))
    prev8 = pl.BlockSpec((8, 768), lambda i: (jnp.maximum(i * (tm // 8) - 1, 0), 0))
    params = (p["g_pre_mix"], p["w_gate"], p["a_ln_g"], p["a_ln_b"], p["a_w_s"], p["a_b_s"],
              p["c_w_conv"], p["w_branch"], p["w_out"], p["g_post_mix"])
    in_specs = [rows(D_MODEL), rows(512), rows(768), prev8, rows(MIX), rows(MIX)]
    in_specs += [_layer_spec(a, l) for a in params]
    return pl.pallas_call(
        functools.partial(_merge_kernel, steps_per_seq=seq // tm), grid=(t // tm,),
        in_specs=in_specs, out_specs=rows(D_MODEL),
        out_shape=jax.ShapeDtypeStruct((t, D_MODEL), F32),
        compiler_params=pltpu.CompilerParams(dimension_semantics=("arbitrary",),
                                             vmem_limit_bytes=VMEM_LIMIT),
        name="merge",
    )(x2, auv, c3, c3, ob, od, *params)


def _mlp_kernel(x_ref, g_ref, w1_ref, w2_ref, gp_ref, o_ref):
    d_ff = w1_ref.shape[1]
    for r0 in range(0, x_ref.shape[0], SUB_MLP):
        rows = slice(r0, r0 + SUB_MLP)
        x = x_ref[rows, :]
        h = _rms(x, g_ref[...]).astype(BF16)
        y = None
        for cidx in range(d_ff // D_MODEL):
            sl = slice(cidx * D_MODEL, (cidx + 1) * D_MODEL)
            a = jnp.maximum(jnp.dot(h, w1_ref[:, sl], preferred_element_type=F32), 0.0)
            part = jnp.dot((a * a).astype(BF16), w2_ref[sl, :], preferred_element_type=F32)
            y = part if y is None else y + part
        o_ref[rows, :] = x + _rms(y, gp_ref[...])


def _mlp(x2, l, p):
    t = x2.shape[0]
    tm = TM_MLP
    rows = pl.BlockSpec((tm, D_MODEL), lambda i: (i, 0))
    params = (p["g_pre_mlp"], p["w_mlp_in"], p["w_mlp_out"], p["g_post_mlp"])
    return pl.pallas_call(
        _mlp_kernel, grid=(t // tm,),
        in_specs=[rows] + [_layer_spec(a, l) for a in params],
        out_specs=rows, out_shape=jax.ShapeDtypeStruct((t, D_MODEL), F32),
        compiler_params=pltpu.CompilerParams(dimension_semantics=("arbitrary",),
                                             vmem_limit_bytes=VMEM_LIMIT),
        name="mlp",
    )(x2, *params)


def _rope_tables(dim, seq):
    inv = 1.0 / (ROPE_THETA ** (jnp.arange(0, dim, 2, dtype=F32) / dim))
    ang = jnp.arange(seq, dtype=F32)[:, None] * inv[None, :]
    return jnp.cos(ang), jnp.sin(ang)


def _position_tables(seq):
    cos_b, sin_b = _rope_tables(B_ROT, seq)
    cos_d, sin_d = _rope_tables(D_ROPE, seq)
    one, zero = jnp.ones((seq, 1), F32), jnp.zeros((seq, 1), F32)
    cb = jnp.concatenate([jnp.tile(cos_b, (1, N_HEADS)), jnp.tile(one, (1, 96))], axis=1)
    sb = jnp.concatenate([jnp.tile(sin_b, (1, N_HEADS)), jnp.tile(zero, (1, 96))], axis=1)
    cq = jnp.concatenate([jnp.tile(one, (1, 64)), cos_d, cos_d, jnp.tile(zero, (1, 32))], axis=1)
    sq = jnp.concatenate([jnp.tile(zero, (1, 64)), sin_d, sin_d, jnp.tile(zero, (1, 32))], axis=1)
    ck = jnp.concatenate([jnp.tile(zero, (1, 64)), cos_d, cos_d, jnp.tile(zero, (1, 32))], axis=1)
    return jnp.concatenate([cb, sb, cq, sq, ck], axis=1)


def _pack_params(g_pre_mix, w_in, a_ln_g, a_ln_b, a_w_s, a_b_s, c_w_conv, d_q_norm_g, d_w_uq, d_kv_norm_g,
                 d_w_ukv, w_branch, w_out, g_post_mix, g_pre_mlp, w_mlp_in, w_mlp_out, g_post_mlp):
    depth = w_in.shape[0]
    wb = w_in.astype(BF16)
    z = lambda n: jnp.zeros((depth, D_MODEL, n), BF16)
    a = wb[..., 0:512]
    bq, bk, bv = wb[..., 512:768], wb[..., 768:1024], wb[..., 1024:1280]
    c = wb[..., 1280:2048]
    cq, ckv, kr = wb[..., 2048:2240], wb[..., 2240:2368], wb[..., 2368:2400]
    kr_sw = jnp.concatenate([-kr[..., 16:32], kr[..., 0:16]], axis=-1)
    w1 = jnp.concatenate([a, _moba_cols(bq), _moba_cols(bk), bv, c,
                          cq[..., 0:128], cq[..., 128:192], kr, z(32),
                          z(64), kr_sw, z(32), ckv], axis=-1)

    nope, x1, x2 = d_w_uq[..., 0:64], d_w_uq[..., 64:80], d_w_uq[..., 80:96]
    zq = lambda n: jnp.zeros((depth, D_Q_LORA, N_HEADS, n), d_w_uq.dtype)
    main = jnp.concatenate([nope, x1, x2, zq(32)], axis=-1).reshape(depth, D_Q_LORA, 512)
    swap = jnp.concatenate([zq(64), -x2, x1, zq(32)], axis=-1).reshape(depth, D_Q_LORA, 512)
    wq = jnp.concatenate([main, swap], axis=-1)
    wq = jnp.concatenate([wq, jnp.zeros((depth, 256 - D_Q_LORA, 1024), wq.dtype)], axis=1).astype(BF16)
    gq = jnp.concatenate([d_q_norm_g, jnp.zeros((depth, 256 - D_Q_LORA), F32)], axis=-1)

    wk = jnp.concatenate([d_w_ukv[..., 0:64], jnp.zeros((depth, D_KV_LORA, N_HEADS, 64), d_w_ukv.dtype)],
                         axis=-1).reshape(depth, D_KV_LORA, 512)
    wv = d_w_ukv[..., 64:128].reshape(depth, D_KV_LORA, MIX)
    row = lambda g: g[:, None, :]
    return {
        "g_pre_mix": row(g_pre_mix), "w1": w1, "gq": row(gq), "gkv": row(d_kv_norm_g), "wq": wq,
        "wkv": jnp.concatenate([wk, wv], axis=-1).astype(BF16),
        "w_gate": wb[..., 2400:],
        "a_ln_g": row(a_ln_g), "a_ln_b": row(a_ln_b),
        "a_w_s": a_w_s.reshape(depth, N_HEADS * A_CHUNK, A_CHUNK),
        "a_b_s": jnp.repeat(jnp.swapaxes(a_b_s, 1, 2), HEAD_DIM, axis=2),
        "c_w_conv": c_w_conv, "w_branch": w_branch.astype(BF16), "w_out": w_out.astype(BF16),
        "g_post_mix": row(g_post_mix), "g_pre_mlp": row(g_pre_mlp),
        "w_mlp_in": w_mlp_in.astype(BF16), "w_mlp_out": w_mlp_out.astype(BF16),
        "g_post_mlp": row(g_post_mlp),
    }


def kernel(x, g_pre_mix, w_in, a_ln_g, a_ln_b, a_w_s, a_b_s, c_w_conv, d_q_norm_g, d_w_uq,
           d_kv_norm_g, d_w_ukv, w_branch, w_out, g_post_mix, g_pre_mlp, w_mlp_in, w_mlp_out,
           g_post_mlp):
    batch, seq, _ = x.shape
    t = batch * seq
    tab = _position_tables(seq)
    p = _pack_params(g_pre_mix, w_in, a_ln_g, a_ln_b, a_w_s, a_b_s, c_w_conv, d_q_norm_g, d_w_uq,
                     d_kv_norm_g, d_w_ukv, w_branch, w_out, g_post_mix, g_pre_mlp, w_mlp_in,
                     w_mlp_out, g_post_mlp)
    x2 = x.reshape(t, D_MODEL)
    for l in range(w_in.shape[0]):
        (auv, bqh, bql, bk, bvt, km, c3, dqt, dk, dvt) = _inproj(x2, l, p, tab, seq)
        km = km.reshape(batch, seq // MOBA_BLOCK, MIX)
        out_b = _moba(bqh, bql, bk, bvt, km, batch, seq)
        out_d = _mla(dqt, dk, dvt, batch, seq)
        x2 = _merge(x2, auv, c3, out_b, out_d, l, p, seq)
        x2 = _mlp(x2, l, p)
    return x2.reshape(batch, seq, D_MODEL)
```

```python
import functools
import math

import jax
import jax.numpy as jnp
from jax import lax
from jax.experimental import pallas as pl
from jax.experimental.pallas import tpu as pltpu

F32 = jnp.float32
BF16 = jnp.bfloat16

D_MODEL = 1024
HEAD_DIM = 64
N_HEADS = 4
MIX = 256
A_CHUNK = 128
MOBA_BLOCK = 256
MOBA_TOPK = 3
B_ROT = 16
D_NOPE, D_ROPE, D_Q_LORA, D_KV_LORA = 64, 32, 192, 128
ROPE_THETA = 500000.0
NORM_EPS = 1e-6
LN_EPS = 1e-5
LOG2E = 1.4426950408889634
MOBA_QSCALE = (HEAD_DIM ** -0.5) * LOG2E
MLA_QSCALE = ((D_NOPE + D_ROPE) ** -0.5) * LOG2E

LANE = 128
TM_IN = 1024
SUB_IN = 512
TM_MERGE = 1024
SUB_MERGE = 256
TM_MLP = 1024
SUB_MLP = 512
TQ = 256
TK = 512
L_ROWS = 16
PIPE_DEPTH = 4
S_SLOTS = PIPE_DEPTH + 2
VMEM_LIMIT = 56 * 1024 * 1024

N1 = 2560
NEG_INF = float("-inf")


def _moba_cols(w):
    lead = w.shape[:-1]
    w4 = w.reshape(lead + (N_HEADS, HEAD_DIM))
    part = lambda a, b: w4[..., a:b].reshape(lead + (N_HEADS * (b - a),))
    return jnp.concatenate([part(0, 8), part(16, 40), part(8, 16), part(40, 64)], axis=-1)


def _moba_head_mask(idx, h):
    l = idx & (LANE - 1)
    return ((l >= 8 * h) & (l < 8 * h + 8)) | ((l >= 32 + 24 * h) & (l < 56 + 24 * h))


def _rms(x, g):
    return x * lax.rsqrt(jnp.mean(x * x, axis=-1, keepdims=True) + NORM_EPS) * g


def _gelu(x):
    return 0.5 * x * (1.0 + jnp.tanh(math.sqrt(2.0 / math.pi) * (x + 0.044715 * (x * x * x))))


def _layer_spec(arr, l):
    nd = arr.ndim - 1
    return pl.BlockSpec((None,) + arr.shape[1:], lambda *_: (l,) + (0,) * nd, pipeline_mode=pl.Buffered(1))


def _inproj_rows(blk, x_ref, tab_ref, g_ref, w1_ref, gq_ref, gkv_ref, wq_ref, wkv_ref,
                 auv_ref, bqh_ref, bql_ref, bk_ref, bvt_ref, km_ref, c3_ref, dqt_ref, dk_ref, dvt_ref):
    tm = SUB_IN
    rows = slice(blk * tm, (blk + 1) * tm)
    h = _rms(x_ref[rows, :], g_ref[...]).astype(BF16)

    def proj(a, b):
        return jnp.dot(h, w1_ref[:, a:b], preferred_element_type=F32)

    d = proj(2048, 2560)
    cq = d[:, 0:256]
    lane = lax.broadcasted_iota(jnp.int32, cq.shape, 1)
    ms = jnp.sum(jnp.where(lane < D_Q_LORA, cq * cq, 0.0), axis=-1, keepdims=True) * (1.0 / D_Q_LORA)
    cqn = (cq * lax.rsqrt(ms + NORM_EPS) * gq_ref[...]).astype(BF16)
    qq = jnp.dot(cqn, wq_ref[...], preferred_element_type=F32)
    cq_t = jnp.concatenate([tab_ref[rows, 256:384]] * N_HEADS, axis=1)
    sq_t = jnp.concatenate([tab_ref[rows, 384:512]] * N_HEADS, axis=1)
    qd = (qq[:, 0:512] * cq_t + qq[:, 512:1024] * sq_t) * MLA_QSCALE
    qdt = qd.T.astype(BF16)

    kvn = _rms(d[:, 384:512], gkv_ref[...]).astype(BF16)
    kv = jnp.dot(kvn, wkv_ref[...], preferred_element_type=F32)
    kr = d[:, 128:256] * tab_ref[rows, 512:640] + d[:, 256:384] * tab_ref[rows, 384:512]
    dk_ref[rows, :] = (kv[:, 0:512] + jnp.concatenate([kr] * N_HEADS, axis=1)).astype(BF16)
    vdt = kv[:, 512:768].T.astype(BF16)
    for j in range(tm // TQ):
        dqt_ref[blk * (tm // TQ) + j] = qdt[:, j * TQ:(j + 1) * TQ]
    for j in range(tm // TK):
        dvt_ref[blk * (tm // TK) + j] = vdt[:, j * TK:(j + 1) * TK]

    cb = tab_ref[rows, 0:128]
    sb = tab_ref[rows, 128:256]

    def rope(y):
        y0, y1 = y[:, 0:LANE], y[:, LANE:2 * LANE]
        return jnp.concatenate([y0 * cb - y1 * sb, y1 * cb + y0 * sb], axis=1)

    q = rope(proj(512, 768)) * MOBA_QSCALE
    k = rope(proj(768, 1024))
    v = proj(1024, 1280)
    qt = q.T
    qt_hi = qt.astype(BF16)
    qt_lo = (qt - qt_hi.astype(F32)).astype(BF16)
    vt = v.T.astype(BF16)
    bk_ref[rows, :] = k.astype(BF16)
    for j in range(tm // TQ):
        bqh_ref[blk * (tm // TQ) + j] = qt_hi[:, j * TQ:(j + 1) * TQ]
        bql_ref[blk * (tm // TQ) + j] = qt_lo[:, j * TQ:(j + 1) * TQ]
    for j in range(tm // TK):
        bvt_ref[blk * (tm // TK) + j] = vt[:, j * TK:(j + 1) * TK]
    for j in range(tm // MOBA_BLOCK):
        kb = k[j * MOBA_BLOCK:(j + 1) * MOBA_BLOCK, :]
        jj = blk * (tm // MOBA_BLOCK) + j
        km_ref[0, jj:jj + 1, :] = jnp.sum(kb, axis=0, keepdims=True) * (1.0 / MOBA_BLOCK)

    auv_ref[rows, :] = proj(0, 512)
    c3_ref[rows, :] = proj(1280, 2048)


def _inproj_kernel(x_ref, *refs):
    for blk in range(x_ref.shape[0] // SUB_IN):
        _inproj_rows(blk, x_ref, *refs)


def _inproj(x2, l, p, tab, seq):
    t = x2.shape[0]
    tm = TM_IN
    nt = t // tm
    spt = seq // tm
    rows = lambda w: pl.BlockSpec((tm, w), lambda i: (i, 0))
    tblk = lambda r: pl.BlockSpec((tm // TQ, r, TQ), lambda i: (i, 0, 0))
    vblk = pl.BlockSpec((tm // TK, MIX, TK), lambda i: (i, 0, 0))
    out_shape = (
        jax.ShapeDtypeStruct((t, 512), F32),
        jax.ShapeDtypeStruct((t // TQ, MIX, TQ), BF16),
        jax.ShapeDtypeStruct((t // TQ, MIX, TQ), BF16),
        jax.ShapeDtypeStruct((t, MIX), BF16),
        jax.ShapeDtypeStruct((t // TK, MIX, TK), BF16),
        jax.ShapeDtypeStruct((nt, tm // MOBA_BLOCK, MIX), F32),
        jax.ShapeDtypeStruct((t, 768), F32),
        jax.ShapeDtypeStruct((t // TQ, 512, TQ), BF16),
        jax.ShapeDtypeStruct((t, 512), BF16),
        jax.ShapeDtypeStruct((t // TK, MIX, TK), BF16),
    )
    out_specs = (
        rows(512), tblk(MIX), tblk(MIX), rows(MIX), vblk,
        pl.BlockSpec((1, tm // MOBA_BLOCK, MIX), lambda i: (i, 0, 0)),
        rows(768), tblk(512), rows(512), vblk,
    )
    params = (p["g_pre_mix"], p["w1"], p["gq"], p["gkv"], p["wq"], p["wkv"])
    in_specs = [rows(D_MODEL), pl.BlockSpec((tm, 640), lambda i: (i % spt, 0))]
    in_specs += [_layer_spec(a, l) for a in params]
    return pl.pallas_call(
        _inproj_kernel, grid=(nt,), in_specs=in_specs, out_specs=out_specs, out_shape=out_shape,
        compiler_params=pltpu.CompilerParams(dimension_semantics=("arbitrary",),
                                             vmem_limit_bytes=VMEM_LIMIT),
        name="inproj",
    )(x2, tab, *params)


def _attend(make_scores, vt_ref, s_ref, o_ref, k_windows):
    hs = [slice(h * HEAD_DIM, (h + 1) * HEAD_DIM) for h in range(N_HEADS)]
    row = lax.broadcasted_iota(jnp.int32, (MOBA_BLOCK, TQ), 0)
    col = lax.broadcasted_iota(jnp.int32, (MOBA_BLOCK, TQ), 1)
    causal = row <= col
    ones = jnp.ones((L_ROWS, TK), BF16)
    units = [(sub, h) for sub in range(2) for h in range(N_HEADS)]
    n_slots = s_ref.shape[0]

    def body(k):
        scorers = [make_scores(sub) for sub in range(2)]
        m, acc, den, out = {}, {}, {}, {}

        def keys_in(sub, w):
            return TK if w < k else (sub + 1) * MOBA_BLOCK

        def score_step(t, u, w):
            sub, h = units[u]
            n = keys_in(sub, w)
            s, bias = scorers[sub](w, h, n)
            if w == k:
                own = jnp.where(causal, s[n - MOBA_BLOCK:n], NEG_INF)
                s = own if n == MOBA_BLOCK else jnp.concatenate([s[0:n - MOBA_BLOCK], own], axis=0)
            s_ref[t % n_slots, 0:n] = s
            mw = None
            for j in range(n // MOBA_BLOCK):
                mb = jnp.max(s[j * MOBA_BLOCK:(j + 1) * MOBA_BLOCK], axis=0, keepdims=True)
                mb = mb if bias is None else mb + bias[j]
                mw = mb if mw is None else jnp.maximum(mw, mb)
            return mw, bias

        def value_step(t, u, w, mw_bias):
            sub, h = units[u]
            n = keys_in(sub, w)
            mw, bias = mw_bias
            m_new = mw if u not in m else jnp.maximum(m[u], mw)
            if bias is None:
                p = jnp.exp2(s_ref[t % n_slots, 0:n] - m_new)
            else:
                p = jnp.concatenate(
                    [jnp.exp2(s_ref[t % n_slots, j * MOBA_BLOCK:(j + 1) * MOBA_BLOCK] - (m_new - bias[j]))
                     for j in range(n // MOBA_BLOCK)], axis=0)
            lw = jnp.sum(p, axis=0, keepdims=True)
            r = jnp.dot(vt_ref[w, hs[h], 0:n], p.astype(BF16), preferred_element_type=F32)
            if u not in acc:
                acc[u], den[u] = r, lw
            else:
                alpha = jnp.exp2(m[u] - m_new)
                acc[u], den[u] = alpha * acc[u] + r, alpha * den[u] + lw
            m[u] = m_new
            if w == last_window:
                out[u] = acc[u] * (1.0 / den[u])

        windows = [k] + list(range(k))
        last_window = windows[-1]
        steps = [(u, w) for u in range(len(units)) for w in windows]
        queue = []
        for t, (u, w) in enumerate(steps):
            queue.append((t, u, w, score_step(t, u, w)))
            if len(queue) > PIPE_DEPTH:
                value_step(*queue.pop(0))
        while queue:
            value_step(*queue.pop(0))
        for sub in range(2):
            tile = jnp.concatenate([out[sub * N_HEADS + h] for h in range(N_HEADS)], axis=0).T
            o_ref[sub * TQ:(sub + 1) * TQ, :] = tile.astype(o_ref.dtype)

    for k in range(vt_ref.shape[0]):
        pl.when(k_windows == k)(functools.partial(body, k))


def _moba_kernel(qh_ref, ql_ref, k_ref, vt_ref, km_ref, o_ref, qm_s, s_ref):
    i2 = pl.program_id(1)
    row = lax.broadcasted_iota(jnp.int32, (MIX, TQ), 0)

    def make_scores(sub):
        i = 2 * i2 + sub
        qt_hi = qh_ref[sub]
        for h in range(N_HEADS):
            qm_s[sub, h] = jnp.where(_moba_head_mask(row, h), qt_hi, jnp.zeros_like(qt_hi))
        km = km_ref[0]
        nb = km.shape[0]
        lane = lax.broadcasted_iota(jnp.int32, km.shape, 1)
        kmh = jnp.concatenate([jnp.where(_moba_head_mask(lane, h), km, 0.0) for h in range(N_HEADS)], axis=0)
        km_hi = kmh.astype(BF16)
        km_lo = (kmh - km_hi.astype(F32)).astype(BF16)
        gate = (jnp.dot(km_hi, qh_ref[sub], preferred_element_type=F32)
                + jnp.dot(km_hi, ql_ref[sub], preferred_element_type=F32)
                + jnp.dot(km_lo, qh_ref[sub], preferred_element_type=F32))

        blk = lax.broadcasted_iota(jnp.int32, (nb, TQ), 0)
        blkf = blk.astype(F32)
        biases = []
        for h in range(N_HEADS):
            g = jnp.where(blk < i, gate[h * nb:(h + 1) * nb, :], NEG_INF)
            bias = jnp.full((nb, TQ), NEG_INF, F32)
            for _ in range(MOBA_TOPK):
                mx = jnp.max(g, axis=0, keepdims=True)
                first = jnp.min(jnp.where(g == mx, blkf, float(nb)), axis=0, keepdims=True)
                hit = blkf == first
                bias = jnp.where(hit & (mx > NEG_INF), 0.0, bias)
                g = jnp.where(hit, NEG_INF, g)
            biases.append(jnp.where(blk == i, 0.0, bias))

        def scores(w, h, n):
            s = jnp.dot(k_ref[w * TK:w * TK + n, :], qm_s[sub, h], preferred_element_type=F32)
            b = biases[h]
            return s, [b[2 * w + j:2 * w + j + 1, :] for j in range(n // MOBA_BLOCK)]

        return scores

    _attend(make_scores, vt_ref, s_ref, o_ref, i2)


def _moba(qh, ql, k, vt, km, batch, seq):
    nq2 = seq // (2 * TQ)
    t = batch * seq
    qspec = pl.BlockSpec((2, MIX, TQ), lambda b, i: (b * nq2 + i, 0, 0))
    return pl.pallas_call(
        _moba_kernel, grid=(batch, nq2),
        in_specs=[qspec, qspec,
                  pl.BlockSpec((seq, MIX), lambda b, i: (b, 0)),
                  pl.BlockSpec((seq // TK, MIX, TK), lambda b, i: (b, 0, 0)),
                  pl.BlockSpec((1, seq // MOBA_BLOCK, MIX), lambda b, i: (b, 0, 0))],
        out_specs=pl.BlockSpec((2 * TQ, MIX), lambda b, i: (b * nq2 + i, 0)),
        out_shape=jax.ShapeDtypeStruct((t, MIX), BF16),
        scratch_shapes=[pltpu.VMEM((2, N_HEADS, MIX, TQ), BF16),
                        pltpu.VMEM((S_SLOTS, TK, TQ), F32)],
        compiler_params=pltpu.CompilerParams(dimension_semantics=("arbitrary", "arbitrary"),
                                             vmem_limit_bytes=VMEM_LIMIT),
        name="moba",
    )(qh, ql, k, vt, km)


def _mla_kernel(qt_ref, k_ref, vt_ref, o_ref, s_ref):
    def make_scores(sub):
        def scores(w, h, n):
            kw = k_ref[w * TK:w * TK + n, h * LANE:(h + 1) * LANE]
            return jnp.dot(kw, qt_ref[sub, h * LANE:(h + 1) * LANE, :], preferred_element_type=F32), None
        return scores

    _attend(make_scores, vt_ref, s_ref, o_ref, pl.program_id(1))


def _mla(qt, k, vt, batch, seq):
    nq2 = seq // (2 * TQ)
    t = batch * seq
    return pl.pallas_call(
        _mla_kernel, grid=(batch, nq2),
        in_specs=[pl.BlockSpec((2, 512, TQ), lambda b, i: (b * nq2 + i, 0, 0)),
                  pl.BlockSpec((seq, 512), lambda b, i: (b, 0)),
                  pl.BlockSpec((seq // TK, MIX, TK), lambda b, i: (b, 0, 0))],
        out_specs=pl.BlockSpec((2 * TQ, MIX), lambda b, i: (b * nq2 + i, 0)),
        out_shape=jax.ShapeDtypeStruct((t, MIX), BF16),
        scratch_shapes=[pltpu.VMEM((S_SLOTS, TK, TQ), F32)],
        compiler_params=pltpu.CompilerParams(dimension_semantics=("arbitrary", "arbitrary"),
                                             vmem_limit_bytes=VMEM_LIMIT),
        name="mla",
    )(qt, k, vt)


def _merge_rows(r0, first_in_seq, x_ref, auv_ref, c3_ref, c3p_ref, ob_ref, od_ref,
                g_ref, wg_ref, lng_ref, lnb_ref, ws, bs_ref, wc_ref, wb_ref, wo_ref, gp_ref, o_ref):
    rows = slice(r0, r0 + SUB_MERGE)
    y_b = jnp.dot(ob_ref[rows, :], wb_ref[1], preferred_element_type=F32)
    y_d = jnp.dot(od_ref[rows, :], wb_ref[3], preferred_element_type=F32)
    x = x_ref[rows, :]
    h = _rms(x, g_ref[...]).astype(BF16)

    u = _gelu(auv_ref[rows, 0:MIX])
    v = _gelu(auv_ref[rows, MIX:2 * MIX])
    mu = jnp.mean(v, axis=-1, keepdims=True)
    vc = v - mu
    var = jnp.mean(vc * vc, axis=-1, keepdims=True)
    v = (vc * lax.rsqrt(var + LN_EPS) * lng_ref[...] + lnb_ref[...]).astype(BF16)
    lane = lax.broadcasted_iota(jnp.int32, (A_CHUNK, MIX), 1)
    svs = []
    for cidx in range(SUB_MERGE // A_CHUNK):
        r = jnp.dot(ws, v[cidx * A_CHUNK:(cidx + 1) * A_CHUNK, :], preferred_element_type=F32)
        sv = jnp.where(lane < 64, r[0:128],
                       jnp.where(lane < 128, r[128:256], jnp.where(lane < 192, r[256:384], r[384:512])))
        svs.append(sv + bs_ref[...])
    out_a = (u * jnp.concatenate(svs, axis=0)).astype(BF16)

    z = c3_ref[rows, MIX:2 * MIX] * c3_ref[rows, 2 * MIX:3 * MIX]
    if r0 == 0:
        zp = c3p_ref[:, MIX:2 * MIX] * c3p_ref[:, 2 * MIX:3 * MIX]
        zp = jnp.where(first_in_seq, 0.0, zp)
    else:
        zp = c3_ref[r0 - 8:r0, MIX:2 * MIX] * c3_ref[r0 - 8:r0, 2 * MIX:3 * MIX]
    zm1, zm2 = zp[7:8, :], zp[6:7, :]
    rowi = lax.broadcasted_iota(jnp.int32, z.shape, 0)
    z1 = jnp.where(rowi == 0, zm1, pltpu.roll(z, 1, axis=0))
    z2 = jnp.where(rowi == 0, zm2, jnp.where(rowi == 1, zm1, pltpu.roll(z, 2, axis=0)))
    conv = wc_ref[0:1, :] * z2 + wc_ref[1:2, :] * z1 + wc_ref[2:3, :] * z
    out_c = (c3_ref[rows, 0:MIX] * conv).astype(BF16)

    def gate(bi):
        logits = jnp.dot(h, wg_ref[:, bi * D_MODEL:(bi + 1) * D_MODEL], preferred_element_type=F32)
        return 1.0 / (1.0 + jnp.exp(-logits))

    merged = gate(1) * y_b + gate(3) * y_d
    merged = merged + gate(0) * jnp.dot(out_a, wb_ref[0], preferred_element_type=F32)
    merged = merged + gate(2) * jnp.dot(out_c, wb_ref[2], preferred_element_type=F32)
    y = jnp.dot(merged.astype(BF16), wo_ref[...], preferred_element_type=F32)
    o_ref[rows, :] = x + _rms(y, gp_ref[...])


def _merge_kernel(x_ref, auv_ref, c3_ref, c3p_ref, ob_ref, od_ref,
                  g_ref, wg_ref, lng_ref, lnb_ref, ws_ref, bs_ref, wc_ref, wb_ref, wo_ref, gp_ref,
                  o_ref, *, steps_per_seq):
    wrow = lax.broadcasted_iota(jnp.int32, ws_ref.shape, 0) & (A_CHUNK - 1)
    wcol = lax.broadcasted_iota(jnp.int32, ws_ref.shape, 1)
    ws = jnp.where(wrow >= wcol, ws_ref[...], 0.0).astype(BF16)
    first_in_seq = pl.program_id(0) % steps_per_seq == 0
    for r0 in range(0, x_ref.shape[0], SUB_MERGE):
        _merge_rows(r0, first_in_seq, x_ref, auv_ref, c3_ref, c3p_ref, ob_ref, od_ref,
                    g_ref, wg_ref, lng_ref, lnb_ref, ws, bs_ref, wc_ref, wb_ref, wo_ref, gp_ref, o_ref)


def _merge(x2, auv, c3, ob, od, l, p, seq):
    t = x2.shape[0]
    tm = TM_MERGE
    rows = lambda w: pl.BlockSpec((tm, w), lambda i: (i, 0))
    prev8 = pl.BlockSpec((8, 768), lambda i: (jnp.maximum(i * (tm // 8) - 1, 0), 0))
    params = (p["g_pre_mix"], p["w_gate"], p["a_ln_g"], p["a_ln_b"], p["a_w_s"], p["a_b_s"],
              p["c_w_conv"], p["w_branch"], p["w_out"], p["g_post_mix"])
    in_specs = [rows(D_MODEL), rows(512), rows(768), prev8, rows(MIX), rows(MIX)]
    in_specs += [_layer_spec(a, l) for a in params]
    return pl.pallas_call(
        functools.partial(_merge_kernel, steps_per_seq=seq // tm), grid=(t // tm,),
        in_specs=in_specs, out_specs=rows(D_MODEL),
        out_shape=jax.ShapeDtypeStruct((t, D_MODEL), F32),
        compiler_params=pltpu.CompilerParams(dimension_semantics=("arbitrary",),
                                             vmem_limit_bytes=VMEM_LIMIT),
        name="merge",
    )(x2, auv, c3, c3, ob, od, *params)


def _mlp_kernel(x_ref, g_ref, w1_ref, w2_ref, gp_ref, o_ref):
    d_ff = w1_ref.shape[1]
    for r0 in range(0, x_ref.shape[0], SUB_MLP):
        rows = slice(r0, r0 + SUB_MLP)
        x = x_ref[rows, :]
        h = _rms(x, g_ref[...]).astype(BF16)
        y = None
        for cidx in range(d_ff // D_MODEL):
            sl = slice(cidx * D_MODEL, (cidx + 1) * D_MODEL)
            a = jnp.maximum(jnp.dot(h, w1_ref[:, sl], preferred_element_type=F32), 0.0)
            part = jnp.dot((a * a).astype(BF16), w2_ref[sl, :], preferred_element_type=F32)
            y = part if y is None else y + part
        o_ref[rows, :] = x + _rms(y, gp_ref[...])


def _mlp(x2, l, p):
    t = x2.shape[0]
    tm = TM_MLP
    rows = pl.BlockSpec((tm, D_MODEL), lambda i: (i, 0))
    params = (p["g_pre_mlp"], p["w_mlp_in"], p["w_mlp_out"], p["g_post_mlp"])
    return pl.pallas_call(
        _mlp_kernel, grid=(t // tm,),
        in_specs=[rows] + [_layer_spec(a, l) for a in params],
        out_specs=rows, out_shape=jax.ShapeDtypeStruct((t, D_MODEL), F32),
        compiler_params=pltpu.CompilerParams(dimension_semantics=("arbitrary",),
                                             vmem_limit_bytes=VMEM_LIMIT),
        name="mlp",
    )(x2, *params)


def _rope_tables(dim, seq):
    inv = 1.0 / (ROPE_THETA ** (jnp.arange(0, dim, 2, dtype=F32) / dim))
    ang = jnp.arange(seq, dtype=F32)[:, None] * inv[None, :]
    return jnp.cos(ang), jnp.sin(ang)


def _position_tables(seq):
    cos_b, sin_b = _rope_tables(B_ROT, seq)
    cos_d, sin_d = _rope_tables(D_ROPE, seq)
    one, zero = jnp.ones((seq, 1), F32), jnp.zeros((seq, 1), F32)
    cb = jnp.concatenate([jnp.tile(cos_b, (1, N_HEADS)), jnp.tile(one, (1, 96))], axis=1)
    sb = jnp.concatenate([jnp.tile(sin_b, (1, N_HEADS)), jnp.tile(zero, (1, 96))], axis=1)
    cq = jnp.concatenate([jnp.tile(one, (1, 64)), cos_d, cos_d, jnp.tile(zero, (1, 32))], axis=1)
    sq = jnp.concatenate([jnp.tile(zero, (1, 64)), sin_d, sin_d, jnp.tile(zero, (1, 32))], axis=1)
    ck = jnp.concatenate([jnp.tile(zero, (1, 64)), cos_d, cos_d, jnp.tile(zero, (1, 32))], axis=1)
    return jnp.concatenate([cb, sb, cq, sq, ck], axis=1)


def _pack_params(g_pre_mix, w_in, a_ln_g, a_ln_b, a_w_s, a_b_s, c_w_conv, d_q_norm_g, d_w_uq, d_kv_norm_g,
                 d_w_ukv, w_branch, w_out, g_post_mix, g_pre_mlp, w_mlp_in, w_mlp_out, g_post_mlp):
    depth = w_in.shape[0]
    wb = w_in.astype(BF16)
    z = lambda n: jnp.zeros((depth, D_MODEL, n), BF16)
    a = wb[..., 0:512]
    bq, bk, bv = wb[..., 512:768], wb[..., 768:1024], wb[..., 1024:1280]
    c = wb[..., 1280:2048]
    cq, ckv, kr = wb[..., 2048:2240], wb[..., 2240:2368], wb[..., 2368:2400]
    kr_sw = jnp.concatenate([-kr[..., 16:32], kr[..., 0:16]], axis=-1)
    w1 = jnp.concatenate([a, _moba_cols(bq), _moba_cols(bk), bv, c,
                          cq[..., 0:128], cq[..., 128:192], kr, z(32),
                          z(64), kr_sw, z(32), ckv], axis=-1)

    nope, x1, x2 = d_w_uq[..., 0:64], d_w_uq[..., 64:80], d_w_uq[..., 80:96]
    zq = lambda n: jnp.zeros((depth, D_Q_LORA, N_HEADS, n), d_w_uq.dtype)
    main = jnp.concatenate([nope, x1, x2, zq(32)], axis=-1).reshape(depth, D_Q_LORA, 512)
    swap = jnp.concatenate([zq(64), -x2, x1, zq(32)], axis=-1).reshape(depth, D_Q_LORA, 512)
    wq = jnp.concatenate([main, swap], axis=-1)
    wq = jnp.concatenate([wq, jnp.zeros((depth, 256 - D_Q_LORA, 1024), wq.dtype)], axis=1).astype(BF16)
    gq = jnp.concatenate([d_q_norm_g, jnp.zeros((depth, 256 - D_Q_LORA), F32)], axis=-1)

    wk = jnp.concatenate([d_w_ukv[..., 0:64], jnp.zeros((depth, D_KV_LORA, N_HEADS, 64), d_w_ukv.dtype)],
                         axis=-1).reshape(depth, D_KV_LORA, 512)
    wv = d_w_ukv[..., 64:128].reshape(depth, D_KV_LORA, MIX)
    row = lambda g: g[:, None, :]
    return {
        "g_pre_mix": row(g_pre_mix), "w1": w1, "gq": row(gq), "gkv": row(d_kv_norm_g), "wq": wq,
        "wkv": jnp.concatenate([wk, wv], axis=-1).astype(BF16),
        "w_gate": wb[..., 2400:],
        "a_ln_g": row(a_ln_g), "a_ln_b": row(a_ln_b),
        "a_w_s": a_w_s.reshape(depth, N_HEADS * A_CHUNK, A_CHUNK),
        "a_b_s": jnp.repeat(jnp.swapaxes(a_b_s, 1, 2), HEAD_DIM, axis=2),
        "c_w_conv": c_w_conv, "w_branch": w_branch.astype(BF16), "w_out": w_out.astype(BF16),
        "g_post_mix": row(g_post_mix), "g_pre_mlp": row(g_pre_mlp),
        "w_mlp_in": w_mlp_in.astype(BF16), "w_mlp_out": w_mlp_out.astype(BF16),
        "g_post_mlp": row(g_post_mlp),
    }


def kernel(x, g_pre_mix, w_in, a_ln_g, a_ln_b, a_w_s, a_b_s, c_w_conv, d_q_norm_g, d_w_uq,
           d_kv_norm_g, d_w_ukv, w_branch, w_out, g_post_mix, g_pre_mlp, w_mlp_in, w_mlp_out,
           g_post_mlp):
    batch, seq, _ = x.shape
    t = batch * seq
    tab = _position_tables(seq)
    p = _pack_params(g_pre_mix, w_in, a_ln_g, a_ln_b, a_w_s, a_b_s, c_w_conv, d_q_norm_g, d_w_uq,
                     d_kv_norm_g, d_w_ukv, w_branch, w_out, g_post_mix, g_pre_mlp, w_mlp_in,
                     w_mlp_out, g_post_mlp)
    x2 = x.reshape(t, D_MODEL)
    for l in range(w_in.shape[0]):
        (auv, bqh, bql, bk, bvt, km, c3, dqt, dk, dvt) = _inproj(x2, l, p, tab, seq)
        km = km.reshape(batch, seq // MOBA_BLOCK, MIX)
        out_b = _moba(bqh, bql, bk, bvt, km, batch, seq)
        out_d = _mla(dqt, dk, dvt, batch, seq)
        x2 = _merge(x2, auv, c3, out_b, out_d, l, p, seq)
        x2 = _mlp(x2, l, p)
    return x2.reshape(batch, seq, D_MODEL)
```
